```python
import jax, jax.numpy as jnp
from jax import lax
import numpy as np

D_MODEL = 1024
BATCH = 4
SEQ = 8192
DEPTH = 1

RET_HEADS = 8
RET_HEAD_DIM = 64
RET_WIDTH = RET_HEADS * RET_HEAD_DIM
RET_CHUNK = 128
MLA_HEADS = 8
MLA_NOPE_DIM = 64
MLA_ROPE_DIM = 32
MLA_V_DIM = 64
MLA_Q_RANK = 256
MLA_KV_RANK = 128
MLA_WIDTH = MLA_HEADS * MLA_V_DIM
MIX_WIDTH = RET_WIDTH + MLA_WIDTH
IN_WIDTH = 4 * RET_WIDTH + MLA_Q_RANK + MLA_KV_RANK + MLA_ROPE_DIM
D_FF = 2816
CONV_WIDTH = 3
Q_BLOCK = 128
ROPE_BASE = 10000.0
EPS = 1e-6

kernel_name = "hybrid_retention_mla_convffn"


def rms_norm(x, w):
    xf = x.astype(jnp.float32)
    y = xf * lax.rsqrt(jnp.mean(xf * xf, axis=-1, keepdims=True) + EPS)
    return (y * w.astype(jnp.float32)).astype(x.dtype)


def rope(x, positions):
    d = x.shape[-1]
    inv_freq = ROPE_BASE ** (-jnp.arange(0, d, 2, dtype=jnp.float32) / d)
    ang = positions.astype(jnp.float32)[..., None] * inv_freq
    if x.ndim == 4:
        ang = ang[:, :, None, :]
    cos, sin = jnp.cos(ang), jnp.sin(ang)
    xf = x.astype(jnp.float32)
    x1, x2 = xf[..., : d // 2], xf[..., d // 2:]
    return jnp.concatenate([x1 * cos - x2 * sin, x1 * sin + x2 * cos], axis=-1).astype(x.dtype)


def retention(q, k, v):
    B, S, H, dk = q.shape
    dv = v.shape[-1]
    C = RET_CHUNK
    N = S // C
    log_gamma = jnp.log1p(-jnp.power(2.0, -5.0 - jnp.arange(H, dtype=jnp.float32)))
    qc = q.astype(jnp.float32).reshape(B, N, C, H, dk)
    kc = k.astype(jnp.float32).reshape(B, N, C, H, dk)
    vc = v.astype(jnp.float32).reshape(B, N, C, H, dv)
    idx = jnp.arange(C, dtype=jnp.float32)
    diff = idx[:, None] - idx[None, :]
    decay_mask = jnp.where(diff >= 0, jnp.exp(log_gamma[:, None, None] * jnp.maximum(diff, 0.0)), 0.0)
    scores = jnp.einsum('bnihd,bnjhd->bnhij', qc, kc) * decay_mask
    o_inner = jnp.einsum('bnhij,bnjhe->bnihe', scores, vc)
    zeta = jnp.exp(log_gamma[:, None] * (C - 1.0 - idx))
    chunk_states = jnp.einsum('bnjhd,hj,bnjhe->nbhde', kc, zeta, vc)
    chunk_decay = jnp.exp(log_gamma * C)[None, :, None, None]

    def step(R, s_n):
        return chunk_decay * R + s_n, R

    _, r_prev = lax.scan(step, jnp.zeros((B, H, dk, dv), jnp.float32), chunk_states)
    r_prev = jnp.moveaxis(r_prev, 0, 1)
    xi = jnp.exp(log_gamma[:, None] * (idx + 1.0))
    o_cross = jnp.einsum('bnihd,bnhde,hi->bnihe', qc, r_prev, xi)
    return (o_inner + o_cross).reshape(B, S, H, dv)


def retention_group(q, k, v, g, positions, gn_w):
    B, S, _ = q.shape
    q = rope(q.reshape(B, S, RET_HEADS, RET_HEAD_DIM), positions)
    k = rope(k.reshape(B, S, RET_HEADS, RET_HEAD_DIM), positions) * (RET_HEAD_DIM ** -0.5)
    v = v.reshape(B, S, RET_HEADS, RET_HEAD_DIM)
    o = retention(q, k, v)
    mu = jnp.mean(o, axis=-1, keepdims=True)
    var = jnp.mean(jnp.square(o - mu), axis=-1, keepdims=True)
    o = ((o - mu) * lax.rsqrt(var + EPS)).reshape(B, S, RET_WIDTH) * gn_w.astype(jnp.float32)
    return (jax.nn.silu(g.astype(jnp.float32)) * o).astype(g.dtype)


def mla_group(c_q, c_kv, k_pe, positions, q_norm_w, w_uq, kv_norm_w, w_ukv):
    B, S, _ = c_q.shape
    H = MLA_HEADS
    q = jnp.einsum('bsr,rf->bsf', rms_norm(c_q, q_norm_w), w_uq).reshape(B, S, H, MLA_NOPE_DIM + MLA_ROPE_DIM)
    q_nope = q[..., :MLA_NOPE_DIM]
    q_pe = rope(q[..., MLA_NOPE_DIM:], positions)
    kv = jnp.einsum('bsr,rf->bsf', rms_norm(c_kv, kv_norm_w), w_ukv).reshape(B, S, H, MLA_NOPE_DIM + MLA_V_DIM)
    k_nope = kv[..., :MLA_NOPE_DIM]
    v = kv[..., MLA_NOPE_DIM:]
    k_pe = rope(k_pe, positions)
    scale = (MLA_NOPE_DIM + MLA_ROPE_DIM) ** -0.5
    N = S // Q_BLOCK
    qn_b = jnp.moveaxis(q_nope.reshape(B, N, Q_BLOCK, H, MLA_NOPE_DIM), 1, 0)
    qp_b = jnp.moveaxis(q_pe.reshape(B, N, Q_BLOCK, H, MLA_ROPE_DIM), 1, 0)
    key_pos = jnp.arange(S)
    neg = jnp.finfo(jnp.float32).min

    def block(args):
        qn, qp, blk = args
        s = (jnp.einsum('bqhd,bkhd->bhqk', qn, k_nope)
             + jnp.einsum('bqhr,bkr->bhqk', qp, k_pe)).astype(jnp.float32) * scale
        q_pos = blk * Q_BLOCK + jnp.arange(Q_BLOCK)
        s = jnp.where(key_pos[None, :] <= q_pos[:, None], s, neg)
        p = jax.nn.softmax(s, axis=-1).astype(v.dtype)
        return jnp.einsum('bhqk,bkhd->bqhd', p, v)

    o = lax.map(block, (qn_b, qp_b, jnp.arange(N)))
    return jnp.moveaxis(o, 0, 1).reshape(B, S, MLA_WIDTH)


def conv_ffn(h, w_up, conv_w, conv_b, w_down):
    S = h.shape[1]
    u = jnp.einsum('bsd,df->bsf', h, w_up)
    up = jnp.pad(u, ((0, 0), (CONV_WIDTH - 1, 0), (0, 0)))
    u = conv_b + sum(conv_w[j] * up[:, j:j + S] for j in range(CONV_WIDTH))
    gate, val = u[..., :D_FF], u[..., D_FF:]
    return jnp.einsum('bsf,fd->bsd', jax.nn.silu(gate) * val, w_down)


def setup_inputs(seed: int = 0) -> dict:
    key = jax.random.key(seed)
    ks = jax.random.split(key, 20)
    f32 = jnp.float32

    def nrm(k, shape, fan_in):
        return jax.random.normal(k, shape, f32) * (fan_in ** -0.5)

    def gain(k, shape):
        return 1.0 + 0.02 * jax.random.normal(k, shape, f32)

    x = jax.random.normal(ks[0], (BATCH, SEQ, D_MODEL), f32)
    offset = jax.random.randint(ks[1], (BATCH, 1), 0, 4096, dtype=jnp.int32)
    positions = (offset + jnp.arange(SEQ, dtype=jnp.int32)[None, :]).astype(jnp.int32)
    return {
        "x": x,
        "positions": positions,
        "attn_norm_w": gain(ks[2], (DEPTH, D_MODEL)),
        "w_in": nrm(ks[3], (DEPTH, D_MODEL, IN_WIDTH), D_MODEL),
        "ret_gn_w": gain(ks[4], (DEPTH, RET_WIDTH)),
        "mla_q_norm_w": gain(ks[5], (DEPTH, MLA_Q_RANK)),
        "w_uq": nrm(ks[6], (DEPTH, MLA_Q_RANK, MLA_HEADS * (MLA_NOPE_DIM + MLA_ROPE_DIM)), MLA_Q_RANK),
        "mla_kv_norm_w": gain(ks[7], (DEPTH, MLA_KV_RANK)),
        "w_ukv": nrm(ks[8], (DEPTH, MLA_KV_RANK, MLA_HEADS * (MLA_NOPE_DIM + MLA_V_DIM)), MLA_KV_RANK),
        "w_out": nrm(ks[9], (DEPTH, MIX_WIDTH, D_MODEL), MIX_WIDTH),
        "ffn_norm_w": gain(ks[10], (DEPTH, D_MODEL)),
        "w_up": nrm(ks[11], (DEPTH, D_MODEL, 2 * D_FF), D_MODEL),
        "conv_w": nrm(ks[12], (DEPTH, CONV_WIDTH, 2 * D_FF), CONV_WIDTH),
        "conv_b": 0.01 * jax.random.normal(ks[13], (DEPTH, 2 * D_FF), f32),
        "w_down": nrm(ks[14], (DEPTH, D_FF, D_MODEL), D_FF),
        "final_norm_w": gain(ks[15], (D_MODEL,)),
    }


def reference(x, positions, attn_norm_w, w_in, ret_gn_w, mla_q_norm_w, w_uq, mla_kv_norm_w, w_ukv,
              w_out, ffn_norm_w, w_up, conv_w, conv_b, w_down, final_norm_w):
    splits = np.cumsum([RET_WIDTH, RET_WIDTH, RET_WIDTH, RET_WIDTH, MLA_Q_RANK, MLA_KV_RANK]).tolist()
    for l in range(DEPTH):
        h = rms_norm(x, attn_norm_w[l])
        proj = jnp.einsum('bsd,df->bsf', h, w_in[l])
        r_q, r_k, r_v, r_g, c_q, c_kv, k_pe = jnp.split(proj, splits, axis=-1)
        y_ret = retention_group(r_q, r_k, r_v, r_g, positions, ret_gn_w[l])
        y_mla = mla_group(c_q, c_kv, k_pe, positions, mla_q_norm_w[l], w_uq[l],
                          mla_kv_norm_w[l], w_ukv[l])
        mixed = jnp.concatenate([y_ret, y_mla.astype(y_ret.dtype)], axis=-1)
        x = x + jnp.einsum('bsm,md->bsd', mixed, w_out[l])
        x = x + conv_ffn(rms_norm(x, ffn_norm_w[l]), w_up[l], conv_w[l], conv_b[l], w_down[l])
    return rms_norm(x, final_norm_w)
```

```python
import functools

import jax
import jax.numpy as jnp
import numpy as np
from jax import lax
from jax.experimental import pallas as pl
from jax.experimental.pallas import tpu as pltpu

F32 = jnp.float32
BF16 = jnp.bfloat16

D_MODEL = 1024
RET_HEADS = 8
RET_HEAD_DIM = 64
RET_WIDTH = RET_HEADS * RET_HEAD_DIM
MLA_HEADS = 8
MLA_NOPE_DIM = 64
MLA_ROPE_DIM = 32
MLA_V_DIM = 64
MLA_Q_RANK = 256
MLA_KV_RANK = 128
MLA_WIDTH = MLA_HEADS * MLA_V_DIM
D_FF = 2816
CONV_WIDTH = 3
ROPE_BASE = 10000.0
EPS = 1e-6

LANES = 128
SUBLANES = 8
HEAD_PAD = LANES
MLA_QK_WIDTH = MLA_HEADS * HEAD_PAD
IN_PAD_WIDTH = 4 * RET_WIDTH + MLA_Q_RANK + MLA_KV_RANK + LANES

TM_IN = 512
RET_CHUNK = 256
TQ = 512
TK = 512
TM_FFN = 512
TF = 256
VMEM_LIMIT = 56 * 1024 * 1024


def _rms(x, w):
    return x * lax.rsqrt(jnp.mean(x * x, axis=-1, keepdims=True) + EPS) * w


def _inproj_kernel(x_ref, cr_ref, sr_ref, cm_ref, sm_ref, anw_ref, win_ref, qnw_ref, wq_ref,
                   kvnw_ref, wk_ref, wv_ref,
                   rq_ref, rk_ref, rv_ref, rg_ref, q_ref, k_ref, v_ref):
    tm = x_ref.shape[1]
    h = _rms(x_ref[0], anw_ref[...]).astype(BF16)

    lane = lax.broadcasted_iota(jnp.int32, (tm, LANES), 1)
    ret_first_half = (lane % RET_HEAD_DIM) < (RET_HEAD_DIM // 2)
    mla_first_half = lane < (MLA_NOPE_DIM + MLA_ROPE_DIM // 2)
    cr, sr, cm, sm = cr_ref[0], sr_ref[0], cm_ref[0], sm_ref[0]

    def rope_ret(p):
        rot = jnp.where(ret_first_half, pltpu.roll(p, LANES - 32, 1), pltpu.roll(p, 32, 1))
        return p * cr + rot * sr

    def rope_mla(p):
        rot = jnp.where(mla_first_half, pltpu.roll(p, LANES - 16, 1), pltpu.roll(p, 16, 1))
        return p * cm + rot * sm

    def seg(lo, width):
        return jnp.dot(h, win_ref[:, lo:lo + width], preferred_element_type=F32)

    pq = seg(0, RET_WIDTH)
    pk = seg(RET_WIDTH, RET_WIDTH)
    for g in range(RET_WIDTH // LANES):
        sl = slice(g * LANES, (g + 1) * LANES)
        rq_ref[0, :, sl] = rope_ret(pq[:, sl]).astype(BF16)
        rk_ref[0, :, sl] = (rope_ret(pk[:, sl]) * (RET_HEAD_DIM ** -0.5)).astype(BF16)
    rv_ref[0] = seg(2 * RET_WIDTH, RET_WIDTH).astype(BF16)
    rg_ref[0] = seg(3 * RET_WIDTH, RET_WIDTH)

    cq = seg(4 * RET_WIDTH, MLA_Q_RANK)
    cqn = _rms(cq, qnw_ref[...]).astype(BF16)
    qf = jnp.dot(cqn, wq_ref[...], preferred_element_type=F32)
    tail = seg(4 * RET_WIDTH + MLA_Q_RANK, MLA_KV_RANK + LANES)
    ckv = tail[:, :MLA_KV_RANK]
    kpe = rope_mla(tail[:, MLA_KV_RANK:])
    ckvn = _rms(ckv, kvnw_ref[...]).astype(BF16)
    kf = jnp.dot(ckvn, wk_ref[...], preferred_element_type=F32)
    v_ref[0] = jnp.dot(ckvn, wv_ref[...], preferred_element_type=F32).astype(BF16)
    scale = (MLA_NOPE_DIM + MLA_ROPE_DIM) ** -0.5
    for hd in range(MLA_HEADS):
        sl = slice(hd * HEAD_PAD, (hd + 1) * HEAD_PAD)
        q_ref[0, :, sl] = (rope_mla(qf[:, sl]) * scale).astype(BF16)
        k_ref[0, :, sl] = (kf[:, sl] + kpe).astype(BF16)


def _inproj(x, tabs, anw, win, qnw, wq, kvnw, wk, wv):
    B, S, D = x.shape
    tm = TM_IN
    grid = (B, S // tm)
    tok = lambda w: pl.BlockSpec((1, tm, w), lambda b, i: (b, i, 0))
    full = lambda a: pl.BlockSpec(a.shape, lambda b, i: (0,) * a.ndim)
    outs = [
        jax.ShapeDtypeStruct((B, S, RET_WIDTH), BF16),
        jax.ShapeDtypeStruct((B, S, RET_WIDTH), BF16),
        jax.ShapeDtypeStruct((B, S, RET_WIDTH), BF16),
        jax.ShapeDtypeStruct((B, S, RET_WIDTH), F32),
        jax.ShapeDtypeStruct((B, S, MLA_QK_WIDTH), BF16),
        jax.ShapeDtypeStruct((B, S, MLA_QK_WIDTH), BF16),
        jax.ShapeDtypeStruct((B, S, MLA_WIDTH), BF16),
    ]
    return pl.pallas_call(
        _inproj_kernel,
        grid=grid,
        in_specs=[tok(D)] + [tok(LANES)] * 4 + [full(a) for a in (anw, win, qnw, wq, kvnw, wk, wv)],
        out_specs=[tok(o.shape[-1]) for o in outs],
        out_shape=outs,
        compiler_params=pltpu.CompilerParams(
            dimension_semantics=("parallel", "parallel"), vmem_limit_bytes=VMEM_LIMIT),
        name="inproj",
    )(x, *tabs, anw, win, qnw, wq, kvnw, wk, wv)


def _retention_kernel(q_ref, k_ref, v_ref, g_ref, dm_ref, xi_ref, zeta_ref, sdec_ref, smask_ref,
                      gnw_ref, o_ref, r_ref):
    C = q_ref.shape[1]

    @pl.when(pl.program_id(1) == 0)
    def _():
        r_ref[...] = jnp.zeros_like(r_ref)

    lane = lax.broadcasted_iota(jnp.int32, (C, LANES), 1)
    first = lane < RET_HEAD_DIM
    inv_n = 1.0 / RET_HEAD_DIM
    for p in range(RET_HEADS // 2):
        sl = slice(p * LANES, (p + 1) * LANES)
        q = q_ref[0, :, sl]
        k = k_ref[0, :, sl]
        v = v_ref[0, :, sl]
        zero = jnp.zeros_like(q)
        inner = []
        for hh, qm in enumerate((jnp.where(first, q, zero), jnp.where(first, zero, q))):
            s = lax.dot_general(qm, k, (((1,), (1,)), ((), ())), preferred_element_type=F32)
            s = (s * dm_ref[2 * p + hh]).astype(BF16)
            inner.append(jnp.dot(s, v, preferred_element_type=F32))
        o = jnp.where(first, inner[0], inner[1])
        r_prev = r_ref[p]
        o = o + jnp.dot(q, r_prev.astype(BF16), preferred_element_type=F32) * xi_ref[p]
        kz = (k.astype(F32) * zeta_ref[p]).T.astype(BF16)
        upd = jnp.dot(kz, v, preferred_element_type=F32)
        r_ref[p] = sdec_ref[p] * r_prev + smask_ref[...] * upd

        s_all = jnp.sum(o, axis=-1, keepdims=True)
        s_1 = jnp.sum(jnp.where(first, o, 0.0), axis=-1, keepdims=True)
        mu = jnp.where(first, s_1, s_all - s_1) * inv_n
        d = o - mu
        d2 = d * d
        v_all = jnp.sum(d2, axis=-1, keepdims=True)
        v_1 = jnp.sum(jnp.where(first, d2, 0.0), axis=-1, keepdims=True)
        var = jnp.where(first, v_1, v_all - v_1) * inv_n
        y = d * lax.rsqrt(var + EPS) * gnw_ref[:, sl]
        g = g_ref[0, :, sl]
        o_ref[0, :, sl] = (g * jax.nn.sigmoid(g) * y).astype(BF16)


def _retention(rq, rk, rv, rg, gnw):
    B, S, W = rq.shape
    C = RET_CHUNK
    H = RET_HEADS
    log_gamma = np.log1p(-np.power(2.0, -5.0 - np.arange(H, dtype=np.float64)))
    idx = np.arange(C, dtype=np.float64)
    diff = idx[:, None] - idx[None, :]
    dm = np.where(diff >= 0, np.exp(log_gamma[:, None, None] * np.maximum(diff, 0.0)), 0.0)
    lane_head = np.arange(W) // RET_HEAD_DIM
    xi = np.exp(log_gamma[lane_head][None, :] * (idx[:, None] + 1.0))
    zeta = np.exp(log_gamma[lane_head][None, :] * (C - 1.0 - idx[:, None]))
    to_pairs = lambda a: np.ascontiguousarray(a.reshape(C, H // 2, LANES).transpose(1, 0, 2))
    row_head = np.arange(LANES) // RET_HEAD_DIM
    smask = (row_head[:, None] == row_head[None, :]).astype(np.float64)
    sdec = np.stack([np.exp(log_gamma[2 * p + row_head] * C)[:, None] * np.ones((1, LANES))
                     for p in range(H // 2)])
    consts = [jnp.asarray(a, F32) for a in (dm, to_pairs(xi), to_pairs(zeta), sdec, smask)]

    tok = lambda: pl.BlockSpec((1, C, W), lambda b, n: (b, n, 0))
    full = lambda a: pl.BlockSpec(a.shape, lambda b, n: (0,) * a.ndim)
    return pl.pallas_call(
        _retention_kernel,
        grid=(B, S // C),
        in_specs=[tok(), tok(), tok(), tok()] + [full(a) for a in consts] + [full(gnw)],
        out_specs=tok(),
        out_shape=jax.ShapeDtypeStruct((B, S, W), BF16),
        scratch_shapes=[pltpu.VMEM((H // 2, LANES, LANES), F32)],
        compiler_params=pltpu.CompilerParams(
            dimension_semantics=("parallel", "arbitrary"), vmem_limit_bytes=VMEM_LIMIT),
        name="retention",
    )(rq, rk, rv, rg, *consts, gnw)


def _mla_kernel(q_ref, k_ref, v_ref, o_ref, m_ref, l_ref, acc_ref):
    i = pl.program_id(1)
    j = pl.program_id(2)
    tq, tk = q_ref.shape[1], k_ref.shape[1]

    @pl.when(j == 0)
    def _():
        m_ref[...] = jnp.full_like(m_ref, -jnp.inf)
        l_ref[...] = jnp.zeros_like(l_ref)
        acc_ref[...] = jnp.zeros_like(acc_ref)

    def step(masked):
        lane = lax.broadcasted_iota(jnp.int32, (tq, LANES), 1)
        first = lane < MLA_V_DIM
        if masked:
            row = lax.broadcasted_iota(jnp.int32, (tq, tk), 0)
            col = lax.broadcasted_iota(jnp.int32, (tq, tk), 1)
            keep = col <= row
        for p in range(MLA_HEADS // 2):
            v = v_ref[0, :, p * LANES:(p + 1) * LANES]
            pv, alpha = [], []
            for hh in range(2):
                hd = 2 * p + hh
                sl = slice(hd * HEAD_PAD, (hd + 1) * HEAD_PAD)
                s = lax.dot_general(q_ref[0, :, sl], k_ref[0, :, sl], (((1,), (1,)), ((), ())),
                                    preferred_element_type=F32)
                if masked:
                    s = jnp.where(keep, s, -jnp.inf)
                m_prev = m_ref[hd]
                m_new = jnp.maximum(m_prev, jnp.max(s, axis=-1, keepdims=True))
                a = jnp.exp(m_prev - m_new)
                pr = jnp.exp(s - m_new[:, :1])
                l_ref[hd] = a * l_ref[hd] + jnp.sum(pr, axis=-1, keepdims=True)
                m_ref[hd] = m_new
                pv.append(jnp.dot(pr.astype(BF16), v, preferred_element_type=F32))
                alpha.append(a)
            acc_ref[p] = jnp.where(first, alpha[0], alpha[1]) * acc_ref[p] + jnp.where(first, pv[0], pv[1])

    @pl.when(j < i)
    def _():
        step(False)

    @pl.when(j == i)
    def _():
        step(True)
        lane = lax.broadcasted_iota(jnp.int32, (tq, LANES), 1)
        first = lane < MLA_V_DIM
        for p in range(MLA_HEADS // 2):
            l = jnp.where(first, l_ref[2 * p], l_ref[2 * p + 1])
            o_ref[0, :, p * LANES:(p + 1) * LANES] = (acc_ref[p] / l).astype(BF16)


def _mla(q, k, v):
    B, S, _ = q.shape
    assert TQ == TK
    grid = (B, S // TQ, S // TK)
    return pl.pallas_call(
        _mla_kernel,
        grid=grid,
        in_specs=[
            pl.BlockSpec((1, TQ, MLA_QK_WIDTH), lambda b, i, j: (b, i, 0)),
            pl.BlockSpec((1, TK, MLA_QK_WIDTH), lambda b, i, j: (b, jnp.minimum(i, j), 0)),
            pl.BlockSpec((1, TK, MLA_WIDTH), lambda b, i, j: (b, jnp.minimum(i, j), 0)),
        ],
        out_specs=pl.BlockSpec((1, TQ, MLA_WIDTH), lambda b, i, j: (b, i, 0)),
        out_shape=jax.ShapeDtypeStruct((B, S, MLA_WIDTH), BF16),
        scratch_shapes=[
            pltpu.VMEM((MLA_HEADS, TQ, LANES), F32),
            pltpu.VMEM((MLA_HEADS, TQ, LANES), F32),
            pltpu.VMEM((MLA_HEADS // 2, TQ, LANES), F32),
        ],
        compiler_params=pltpu.CompilerParams(
            dimension_semantics=("parallel", "parallel", "arbitrary"), vmem_limit_bytes=VMEM_LIMIT),
        name="mla_attention",
    )(q, k, v)


def _ffn_kernel(x_ref, yr_ref, ym_ref, wo_ref, fnw_ref, wup_ref, cw_ref, cb_ref, wdn_ref, onw_ref,
                o_ref, carry_ref, ubuf_ref, act_ref):
    tm = x_ref.shape[1]
    halo = SUBLANES

    @pl.when(pl.program_id(1) == 0)
    def _():
        carry_ref[...] = jnp.zeros_like(carry_ref)

    x1 = (x_ref[0]
          + jnp.dot(yr_ref[0], wo_ref[:RET_WIDTH, :], preferred_element_type=F32)
          + jnp.dot(ym_ref[0], wo_ref[RET_WIDTH:, :], preferred_element_type=F32))
    h = _rms(x1, fnw_ref[...]).astype(BF16)

    for f in range(D_FF // TF):
        for half in range(2):
            lo = half * D_FF + f * TF
            dst = slice(half * TF, (half + 1) * TF)
            ubuf_ref[:halo, dst] = carry_ref[:, lo:lo + TF]
            ubuf_ref[halo:, dst] = jnp.dot(h, wup_ref[:, lo:lo + TF], preferred_element_type=F32)
            carry_ref[:, lo:lo + TF] = ubuf_ref[tm:, dst]
        conv = []
        for half in range(2):
            lo = half * D_FF + f * TF
            dst = slice(half * TF, (half + 1) * TF)
            c = cb_ref[:, lo:lo + TF]
            for t in range(CONV_WIDTH):
                off = halo - (CONV_WIDTH - 1) + t
                c = c + cw_ref[t:t + 1, lo:lo + TF] * ubuf_ref[off:off + tm, dst]
            conv.append(c)
        gate, val = conv
        act_ref[:, f * TF:(f + 1) * TF] = (gate * jax.nn.sigmoid(gate) * val).astype(BF16)

    x2 = x1 + jnp.dot(act_ref[...], wdn_ref[...], preferred_element_type=F32)
    o_ref[0] = _rms(x2, onw_ref[...])


def _ffn(x, y_ret, y_mla, wo, fnw, wup, cw, cb, wdn, onw):
    B, S, D = x.shape
    tm = TM_FFN
    tok = lambda w: pl.BlockSpec((1, tm, w), lambda b, i: (b, i, 0))
    full = lambda a: pl.BlockSpec(a.shape, lambda b, i: (0,) * a.ndim, pipeline_mode=pl.Buffered(1))
    return pl.pallas_call(
        _ffn_kernel,
        grid=(B, S // tm),
        in_specs=[tok(D), tok(RET_WIDTH), tok(MLA_WIDTH)] + [full(a) for a in (wo, fnw, wup, cw, cb, wdn, onw)],
        out_specs=tok(D),
        out_shape=jax.ShapeDtypeStruct((B, S, D), F32),
        scratch_shapes=[
            pltpu.VMEM((SUBLANES, 2 * D_FF), F32),
            pltpu.VMEM((tm + SUBLANES, 2 * TF), F32),
            pltpu.VMEM((tm, D_FF), BF16),
        ],
        compiler_params=pltpu.CompilerParams(
            dimension_semantics=("parallel", "arbitrary"), vmem_limit_bytes=VMEM_LIMIT),
        name="outproj_ffn",
    )(x, y_ret, y_mla, wo, fnw, wup, cw, cb, wdn, onw)


def _rope_tables(positions):
    pos = positions.astype(F32)[..., None]
    lane = np.arange(LANES)

    def table(d, active, first_half, freq_idx):
        inv_freq = ROPE_BASE ** (-jnp.arange(0, d, 2, dtype=F32) / d)
        ang = pos * inv_freq
        cos, sin = jnp.cos(ang), jnp.sin(ang)
        cos_l = jnp.where(active, jnp.take(cos, freq_idx, axis=-1), 1.0)
        sin_l = jnp.where(active, jnp.take(sin, freq_idx, axis=-1), 0.0) * np.where(first_half, -1.0, 1.0)
        return cos_l.astype(F32), sin_l.astype(F32)

    half_r = RET_HEAD_DIM // 2
    cr, sr = table(RET_HEAD_DIM, np.ones(LANES, bool), (lane % RET_HEAD_DIM) < half_r, lane % half_r)
    half_m = MLA_ROPE_DIM // 2
    in_rope = (lane >= MLA_NOPE_DIM) & (lane < MLA_NOPE_DIM + MLA_ROPE_DIM)
    cm, sm = table(MLA_ROPE_DIM, in_rope, lane < MLA_NOPE_DIM + half_m, (lane - MLA_NOPE_DIM) % half_m)
    return cr, sr, cm, sm


def _layer(x, tabs, attn_norm_w, w_in, ret_gn_w, mla_q_norm_w, w_uq, mla_kv_norm_w, w_ukv,
           w_out, ffn_norm_w, w_up, conv_w, conv_b, w_down, out_norm_w):
    D = x.shape[-1]
    qk = MLA_NOPE_DIM + MLA_ROPE_DIM
    kpe_cols = jnp.zeros((D, LANES), F32).at[:, MLA_NOPE_DIM:qk].set(w_in[:, -MLA_ROPE_DIM:])
    win = jnp.concatenate([w_in[:, :-MLA_ROPE_DIM], kpe_cols], axis=1).astype(BF16)
    wq = jnp.pad(w_uq.reshape(MLA_Q_RANK, MLA_HEADS, qk), ((0, 0), (0, 0), (0, HEAD_PAD - qk)))
    wq = wq.reshape(MLA_Q_RANK, MLA_QK_WIDTH).astype(BF16)
    wkv = w_ukv.reshape(MLA_KV_RANK, MLA_HEADS, MLA_NOPE_DIM + MLA_V_DIM)
    wk = jnp.pad(wkv[:, :, :MLA_NOPE_DIM], ((0, 0), (0, 0), (0, HEAD_PAD - MLA_NOPE_DIM)))
    wk = wk.reshape(MLA_KV_RANK, MLA_QK_WIDTH).astype(BF16)
    wv = wkv[:, :, MLA_NOPE_DIM:].reshape(MLA_KV_RANK, MLA_WIDTH).astype(BF16)
    row = lambda a: a.reshape(1, -1).astype(F32)

    rq, rk, rv, rg, q, k, v = _inproj(x, tabs, row(attn_norm_w), win, row(mla_q_norm_w), wq,
                                      row(mla_kv_norm_w), wk, wv)
    y_ret = _retention(rq, rk, rv, rg, row(ret_gn_w))
    y_mla = _mla(q, k, v)
    return _ffn(x, y_ret, y_mla, w_out.astype(BF16), row(ffn_norm_w), w_up.astype(BF16),
                conv_w.astype(F32), row(conv_b), w_down.astype(BF16), row(out_norm_w))


def kernel(x, positions, attn_norm_w, w_in, ret_gn_w, mla_q_norm_w, w_uq, mla_kv_norm_w, w_ukv,
           w_out, ffn_norm_w, w_up, conv_w, conv_b, w_down, final_norm_w):
    depth = w_in.shape[0]
    assert depth == 1, "the final RMSNorm is fused into the (single) layer's FFN kernel"
    tabs = _rope_tables(positions)
    return _layer(x, tabs, attn_norm_w[0], w_in[0], ret_gn_w[0], mla_q_norm_w[0], w_uq[0],
                  mla_kv_norm_w[0], w_ukv[0], w_out[0], ffn_norm_w[0], w_up[0], conv_w[0],
                  conv_b[0], w_down[0], final_norm_w)
```

```python
import functools

import jax
import jax.numpy as jnp
import numpy as np
from jax import lax
from jax.experimental import pallas as pl
from jax.experimental.pallas import tpu as pltpu

F32 = jnp.float32
BF16 = jnp.bfloat16

D_MODEL = 1024
RET_HEADS = 8
RET_HEAD_DIM = 64
RET_WIDTH = RET_HEADS * RET_HEAD_DIM
MLA_HEADS = 8
MLA_NOPE_DIM = 64
MLA_ROPE_DIM = 32
MLA_V_DIM = 64
MLA_Q_RANK = 256
MLA_KV_RANK = 128
MLA_WIDTH = MLA_HEADS * MLA_V_DIM
D_FF = 2816
CONV_WIDTH = 3
ROPE_BASE = 10000.0
EPS = 1e-6
LOG2_E = 1.4426950408889634

LANES = 128
SUBLANES = 8
HEAD_PAD = LANES
MLA_QK_WIDTH = MLA_HEADS * HEAD_PAD
IN_PAD_WIDTH = 4 * RET_WIDTH + MLA_Q_RANK + MLA_KV_RANK + LANES

TM_IN = 512
RET_CHUNK = 256
TQ = 1024
TK = 512
TM_FFN = 512
TF = 256
VMEM_LIMIT = 56 * 1024 * 1024


def _rms(x, w):
    return x * lax.rsqrt(jnp.mean(x * x, axis=-1, keepdims=True) + EPS) * w


def _inproj_kernel(x_ref, cr_ref, sr_ref, cm_ref, sm_ref, anw_ref, win_ref, qnw_ref, wq_ref,
                   kvnw_ref, wk_ref, wv_ref,
                   rq_ref, rk_ref, rv_ref, rg_ref, q_ref, k_ref, v_ref):
    tm = x_ref.shape[1]
    h = _rms(x_ref[0], anw_ref[...]).astype(BF16)

    lane = lax.broadcasted_iota(jnp.int32, (tm, LANES), 1)
    ret_first_half = (lane % RET_HEAD_DIM) < (RET_HEAD_DIM // 2)
    mla_first_half = lane < (MLA_NOPE_DIM + MLA_ROPE_DIM // 2)
    cr, sr, cm, sm = cr_ref[0], sr_ref[0], cm_ref[0], sm_ref[0]

    def rope_ret(p):
        rot = jnp.where(ret_first_half, pltpu.roll(p, LANES - 32, 1), pltpu.roll(p, 32, 1))
        return p * cr + rot * sr

    def rope_mla(p):
        rot = jnp.where(mla_first_half, pltpu.roll(p, LANES - 16, 1), pltpu.roll(p, 16, 1))
        return p * cm + rot * sm

    def seg(lo, width):
        return jnp.dot(h, win_ref[:, lo:lo + width], preferred_element_type=F32)

    pq = seg(0, RET_WIDTH)
    pk = seg(RET_WIDTH, RET_WIDTH)
    for g in range(RET_WIDTH // LANES):
        sl = slice(g * LANES, (g + 1) * LANES)
        rq_ref[0, :, sl] = rope_ret(pq[:, sl]).astype(BF16)
        rk_ref[0, :, sl] = (rope_ret(pk[:, sl]) * (RET_HEAD_DIM ** -0.5)).astype(BF16)
    rv_ref[0] = seg(2 * RET_WIDTH, RET_WIDTH).astype(BF16)
    rg_ref[0] = seg(3 * RET_WIDTH, RET_WIDTH)

    cq = seg(4 * RET_WIDTH, MLA_Q_RANK)
    cqn = _rms(cq, qnw_ref[...]).astype(BF16)
    qf = jnp.dot(cqn, wq_ref[...], preferred_element_type=F32)
    tail = seg(4 * RET_WIDTH + MLA_Q_RANK, MLA_KV_RANK + LANES)
    ckv = tail[:, :MLA_KV_RANK]
    kpe = rope_mla(tail[:, MLA_KV_RANK:])
    ckvn = _rms(ckv, kvnw_ref[...]).astype(BF16)
    kf = jnp.dot(ckvn, wk_ref[...], preferred_element_type=F32)
    vf = jnp.dot(ckvn, wv_ref[...], preferred_element_type=F32)
    scale = (MLA_NOPE_DIM + MLA_ROPE_DIM) ** -0.5 * LOG2_E
    for hd in range(MLA_HEADS):
        sl = slice(hd * HEAD_PAD, (hd + 1) * HEAD_PAD)
        q_ref[0, :, sl] = (rope_mla(qf[:, sl]) * scale).astype(BF16)
        k_ref[0, :, sl] = (kf[:, sl] + kpe).astype(BF16)
        v_ref[0, :, sl] = jnp.where(lane < MLA_V_DIM, vf[:, sl], 1.0).astype(BF16)


def _inproj(x, tabs, anw, win, qnw, wq, kvnw, wk, wv):
    B, S, D = x.shape
    tm = TM_IN
    grid = (B, S // tm)
    tok = lambda w: pl.BlockSpec((1, tm, w), lambda b, i: (b, i, 0))
    full = lambda a: pl.BlockSpec(a.shape, lambda b, i: (0,) * a.ndim)
    outs = [
        jax.ShapeDtypeStruct((B, S, RET_WIDTH), BF16),
        jax.ShapeDtypeStruct((B, S, RET_WIDTH), BF16),
        jax.ShapeDtypeStruct((B, S, RET_WIDTH), BF16),
        jax.ShapeDtypeStruct((B, S, RET_WIDTH), F32),
        jax.ShapeDtypeStruct((B, S, MLA_QK_WIDTH), BF16),
        jax.ShapeDtypeStruct((B, S, MLA_QK_WIDTH), BF16),
        jax.ShapeDtypeStruct((B, S, MLA_QK_WIDTH), BF16),
    ]
    return pl.pallas_call(
        _inproj_kernel,
        grid=grid,
        in_specs=[tok(D)] + [tok(LANES)] * 4 + [full(a) for a in (anw, win, qnw, wq, kvnw, wk, wv)],
        out_specs=[tok(o.shape[-1]) for o in outs],
        out_shape=outs,
        compiler_params=pltpu.CompilerParams(
            dimension_semantics=("parallel", "parallel"), vmem_limit_bytes=VMEM_LIMIT),
        name="inproj",
    )(x, *tabs, anw, win, qnw, wq, kvnw, wk, wv)


def _retention_kernel(q_ref, k_ref, v_ref, g_ref, dm_ref, xi_ref, zeta_ref, sdec_ref, smask_ref,
                      gnw_ref, o_ref, r_ref):
    C = q_ref.shape[1]

    @pl.when(pl.program_id(1) == 0)
    def _():
        r_ref[...] = jnp.zeros_like(r_ref)

    lane = lax.broadcasted_iota(jnp.int32, (C, LANES), 1)
    first = lane < RET_HEAD_DIM
    inv_n = 1.0 / RET_HEAD_DIM
    for p in range(RET_HEADS // 2):
        sl = slice(p * LANES, (p + 1) * LANES)
        q = q_ref[0, :, sl]
        k = k_ref[0, :, sl]
        v = v_ref[0, :, sl]
        zero = jnp.zeros_like(q)
        inner = []
        for hh, qm in enumerate((jnp.where(first, q, zero), jnp.where(first, zero, q))):
            s = lax.dot_general(qm, k, (((1,), (1,)), ((), ())), preferred_element_type=F32)
            s = (s * dm_ref[2 * p + hh]).astype(BF16)
            inner.append(jnp.dot(s, v, preferred_element_type=F32))
        o = jnp.where(first, inner[0], inner[1])
        r_prev = r_ref[p]
        o = o + jnp.dot(q, r_prev.astype(BF16), preferred_element_type=F32) * xi_ref[p]
        kz = (k.astype(F32) * zeta_ref[p]).T.astype(BF16)
        upd = jnp.dot(kz, v, preferred_element_type=F32)
        r_ref[p] = sdec_ref[p] * r_prev + smask_ref[...] * upd

        s_all = jnp.sum(o, axis=-1, keepdims=True)
        s_1 = jnp.sum(jnp.where(first, o, 0.0), axis=-1, keepdims=True)
        mu = jnp.where(first, s_1, s_all - s_1) * inv_n
        d = o - mu
        d2 = d * d
        v_all = jnp.sum(d2, axis=-1, keepdims=True)
        v_1 = jnp.sum(jnp.where(first, d2, 0.0), axis=-1, keepdims=True)
        var = jnp.where(first, v_1, v_all - v_1) * inv_n
        y = d * lax.rsqrt(var + EPS) * gnw_ref[:, sl]
        g = g_ref[0, :, sl]
        o_ref[0, :, sl] = (g * jax.nn.sigmoid(g) * y).astype(BF16)


def _retention(rq, rk, rv, rg, gnw):
    B, S, W = rq.shape
    C = RET_CHUNK
    H = RET_HEADS
    log_gamma = np.log1p(-np.power(2.0, -5.0 - np.arange(H, dtype=np.float64)))
    idx = np.arange(C, dtype=np.float64)
    diff = idx[:, None] - idx[None, :]
    dm = np.where(diff >= 0, np.exp(log_gamma[:, None, None] * np.maximum(diff, 0.0)), 0.0)
    lane_head = np.arange(W) // RET_HEAD_DIM
    xi = np.exp(log_gamma[lane_head][None, :] * (idx[:, None] + 1.0))
    zeta = np.exp(log_gamma[lane_head][None, :] * (C - 1.0 - idx[:, None]))
    to_pairs = lambda a: np.ascontiguousarray(a.reshape(C, H // 2, LANES).transpose(1, 0, 2))
    row_head = np.arange(LANES) // RET_HEAD_DIM
    smask = (row_head[:, None] == row_head[None, :]).astype(np.float64)
    sdec = np.stack([np.exp(log_gamma[2 * p + row_head] * C)[:, None] * np.ones((1, LANES))
                     for p in range(H // 2)])
    consts = [jnp.asarray(a, F32) for a in (dm, to_pairs(xi), to_pairs(zeta), sdec, smask)]

    tok = lambda: pl.BlockSpec((1, C, W), lambda b, n: (b, n, 0))
    full = lambda a: pl.BlockSpec(a.shape, lambda b, n: (0,) * a.ndim)
    return pl.pallas_call(
        _retention_kernel,
        grid=(B, S // C),
        in_specs=[tok(), tok(), tok(), tok()] + [full(a) for a in consts] + [full(gnw)],
        out_specs=tok(),
        out_shape=jax.ShapeDtypeStruct((B, S, W), BF16),
        scratch_shapes=[pltpu.VMEM((H // 2, LANES, LANES), F32)],
        compiler_params=pltpu.CompilerParams(
            dimension_semantics=("parallel", "arbitrary"), vmem_limit_bytes=VMEM_LIMIT),
        name="retention",
    )(rq, rk, rv, rg, *consts, gnw)


def _mla_kernel(qi_ref, kj_ref, bs_ref, q_ref, k_ref, v_ref, bias_ref, o_ref, m_ref, acc_ref):
    t = pl.program_id(1)
    i = qi_ref[t]
    j = kj_ref[t]
    tq, tk = q_ref.shape[1], k_ref.shape[1]
    kt_per_qt = tq // tk

    @pl.when(j == 0)
    def _():
        m_ref[...] = jnp.full_like(m_ref, -jnp.inf)
        acc_ref[...] = jnp.zeros_like(acc_ref)

    def step(diagonal):
        def scores(hd):
            sl = slice(hd * HEAD_PAD, (hd + 1) * HEAD_PAD)
            s = lax.dot_general(q_ref[0, :, sl], k_ref[0, :, sl], (((1,), (1,)), ((), ())),
                                preferred_element_type=F32)
            return s + bias_ref[0] if diagonal else s

        s_next = scores(0)
        for hd in range(MLA_HEADS):
            sl = slice(hd * HEAD_PAD, (hd + 1) * HEAD_PAD)
            s = s_next
            if hd + 1 < MLA_HEADS:
                s_next = scores(hd + 1)
            m_prev = m_ref[hd]
            m_new = jnp.maximum(m_prev, jnp.max(s, axis=-1, keepdims=True))
            a = jnp.exp2(m_prev - m_new)
            pr = jnp.exp2(s - m_new[:, :1]).astype(BF16)
            m_ref[hd] = m_new
            acc_ref[hd] = a * acc_ref[hd] + jnp.dot(pr, v_ref[0, :, sl], preferred_element_type=F32)

    @pl.when(j < kt_per_qt * i)
    def _():
        step(False)

    @pl.when(j >= kt_per_qt * i)
    def _():
        step(True)

    @pl.when(j == kt_per_qt * i + kt_per_qt - 1)
    def _():
        lane = lax.broadcasted_iota(jnp.int32, (tq, LANES), 1)
        first = lane < MLA_V_DIM
        for p in range(MLA_HEADS // 2):
            a0, a1 = acc_ref[2 * p], acc_ref[2 * p + 1]
            o0 = a0 / pltpu.roll(a0, MLA_V_DIM, 1)
            o1 = a1 / pltpu.roll(a1, MLA_V_DIM, 1)
            o_ref[0, :, p * LANES:(p + 1) * LANES] = jnp.where(
                first, o0, pltpu.roll(o1, MLA_V_DIM, 1)).astype(BF16)


def _mla(q, k, v):
    B, S, _ = q.shape
    kt_per_qt = TQ // TK
    pairs = [(i, j) for i in range(S // TQ) for j in range(kt_per_qt * (i + 1))]
    qi = jnp.asarray(np.array([p[0] for p in pairs], np.int32))
    kj = jnp.asarray(np.array([p[1] for p in pairs], np.int32))
    bs = jnp.asarray(np.array([max(p[1] - kt_per_qt * p[0], 0) for p in pairs], np.int32))
    row = np.arange(TQ)[None, :, None]
    col = np.arange(TK)[None, None, :] + TK * np.arange(kt_per_qt)[:, None, None]
    bias = jnp.asarray(np.where(col <= row, 0.0, -np.inf), F32)
    grid_spec = pltpu.PrefetchScalarGridSpec(
        num_scalar_prefetch=3,
        grid=(B, len(pairs)),
        in_specs=[
            pl.BlockSpec((1, TQ, MLA_QK_WIDTH), lambda b, t, qi, kj, bs: (b, qi[t], 0)),
            pl.BlockSpec((1, TK, MLA_QK_WIDTH), lambda b, t, qi, kj, bs: (b, kj[t], 0)),
            pl.BlockSpec((1, TK, MLA_QK_WIDTH), lambda b, t, qi, kj, bs: (b, kj[t], 0)),
            pl.BlockSpec((1, TQ, TK), lambda b, t, qi, kj, bs: (bs[t], 0, 0)),
        ],
        out_specs=pl.BlockSpec((1, TQ, MLA_WIDTH), lambda b, t, qi, kj, bs: (b, qi[t], 0)),
        scratch_shapes=[
            pltpu.VMEM((MLA_HEADS, TQ, LANES), F32),
            pltpu.VMEM((MLA_HEADS, TQ, LANES), F32),
        ],
    )
    return pl.pallas_call(
        _mla_kernel,
        grid_spec=grid_spec,
        out_shape=jax.ShapeDtypeStruct((B, S, MLA_WIDTH), BF16),
        compiler_params=pltpu.CompilerParams(
            dimension_semantics=("parallel", "arbitrary"), vmem_limit_bytes=VMEM_LIMIT),
        name="mla_attention",
    )(qi, kj, bs, q, k, v, bias)


def _ffn_kernel(x_ref, yr_ref, ym_ref, wo_ref, fnw_ref, wup_ref, cw_ref, cb_ref, wdn_ref, onw_ref,
                o_ref, carry_ref, ubuf_ref, act_ref):
    tm = x_ref.shape[1]
    halo = SUBLANES

    @pl.when(pl.program_id(1) == 0)
    def _():
        carry_ref[...] = jnp.zeros_like(carry_ref)

    x1 = (x_ref[0]
          + jnp.dot(yr_ref[0], wo_ref[:RET_WIDTH, :], preferred_element_type=F32)
          + jnp.dot(ym_ref[0], wo_ref[RET_WIDTH:, :], preferred_element_type=F32))
    h = _rms(x1, fnw_ref[...]).astype(BF16)

    for f in range(D_FF // TF):
        for half in range(2):
            lo = half * D_FF + f * TF
            dst = slice(half * TF, (half + 1) * TF)
            ubuf_ref[:halo, dst] = carry_ref[:, lo:lo + TF]
            ubuf_ref[halo:, dst] = jnp.dot(h, wup_ref[:, lo:lo + TF], preferred_element_type=F32)
            carry_ref[:, lo:lo + TF] = ubuf_ref[tm:, dst]
        conv = []
        for half in range(2):
            lo = half * D_FF + f * TF
            dst = slice(half * TF, (half + 1) * TF)
            c = cb_ref[:, lo:lo + TF]
            for t in range(CONV_WIDTH):
                off = halo - (CONV_WIDTH - 1) + t
                c = c + cw_ref[t:t + 1, lo:lo + TF] * ubuf_ref[off:off + tm, dst]
            conv.append(c)
        gate, val = conv
        act_ref[:, f * TF:(f + 1) * TF] = (gate * jax.nn.sigmoid(gate) * val).astype(BF16)

    x2 = x1 + jnp.dot(act_ref[...], wdn_ref[...], preferred_element_type=F32)
    o_ref[0] = _rms(x2, onw_ref[...])


def _ffn(x, y_ret, y_mla, wo, fnw, wup, cw, cb, wdn, onw):
    B, S, D = x.shape
    tm = TM_FFN
    tok = lambda w: pl.BlockSpec((1, tm, w), lambda b, i: (b, i, 0))
    full = lambda a: pl.BlockSpec(a.shape, lambda b, i: (0,) * a.ndim, pipeline_mode=pl.Buffered(1))
    return pl.pallas_call(
        _ffn_kernel,
        grid=(B, S // tm),
        in_specs=[tok(D), tok(RET_WIDTH), tok(MLA_WIDTH)] + [full(a) for a in (wo, fnw, wup, cw, cb, wdn, onw)],
        out_specs=tok(D),
        out_shape=jax.ShapeDtypeStruct((B, S, D), F32),
        scratch_shapes=[
            pltpu.VMEM((SUBLANES, 2 * D_FF), F32),
            pltpu.VMEM((tm + SUBLANES, 2 * TF), F32),
            pltpu.VMEM((tm, D_FF), BF16),
        ],
        compiler_params=pltpu.CompilerParams(
            dimension_semantics=("parallel", "arbitrary"), vmem_limit_bytes=VMEM_LIMIT),
        name="outproj_ffn",
    )(x, y_ret, y_mla, wo, fnw, wup, cw, cb, wdn, onw)


def _rope_tables(positions):
    pos = positions.astype(F32)[..., None]
    lane = np.arange(LANES)

    def table(d, active, first_half, freq_idx):
        inv_freq = ROPE_BASE ** (-jnp.arange(0, d, 2, dtype=F32) / d)
        ang = pos * inv_freq
        cos, sin = jnp.cos(ang), jnp.sin(ang)
        cos_l = jnp.where(active, jnp.take(cos, freq_idx, axis=-1), 1.0)
        sin_l = jnp.where(active, jnp.take(sin, freq_idx, axis=-1), 0.0) * np.where(first_half, -1.0, 1.0)
        return cos_l.astype(F32), sin_l.astype(F32)

    half_r = RET_HEAD_DIM // 2
    cr, sr = table(RET_HEAD_DIM, np.ones(LANES, bool), (lane % RET_HEAD_DIM) < half_r, lane % half_r)
    half_m = MLA_ROPE_DIM // 2
    in_rope = (lane >= MLA_NOPE_DIM) & (lane < MLA_NOPE_DIM + MLA_ROPE_DIM)
    cm, sm = table(MLA_ROPE_DIM, in_rope, lane < MLA_NOPE_DIM + half_m, (lane - MLA_NOPE_DIM) % half_m)
    return cr, sr, cm, sm


def _layer(x, tabs, attn_norm_w, w_in, ret_gn_w, mla_q_norm_w, w_uq, mla_kv_norm_w, w_ukv,
           w_out, ffn_norm_w, w_up, conv_w, conv_b, w_down, out_norm_w):
    D = x.shape[-1]
    qk = MLA_NOPE_DIM + MLA_ROPE_DIM
    kpe_cols = jnp.zeros((D, LANES), F32).at[:, MLA_NOPE_DIM:qk].set(w_in[:, -MLA_ROPE_DIM:])
    win = jnp.concatenate([w_in[:, :-MLA_ROPE_DIM], kpe_cols], axis=1).astype(BF16)
    wq = jnp.pad(w_uq.reshape(MLA_Q_RANK, MLA_HEADS, qk), ((0, 0), (0, 0), (0, HEAD_PAD - qk)))
    wq = wq.reshape(MLA_Q_RANK, MLA_QK_WIDTH).astype(BF16)
    wkv = w_ukv.reshape(MLA_KV_RANK, MLA_HEADS, MLA_NOPE_DIM + MLA_V_DIM)
    wk = jnp.pad(wkv[:, :, :MLA_NOPE_DIM], ((0, 0), (0, 0), (0, HEAD_PAD - MLA_NOPE_DIM)))
    wk = wk.reshape(MLA_KV_RANK, MLA_QK_WIDTH).astype(BF16)
    wv = jnp.pad(wkv[:, :, MLA_NOPE_DIM:], ((0, 0), (0, 0), (0, HEAD_PAD - MLA_V_DIM)))
    wv = wv.reshape(MLA_KV_RANK, MLA_QK_WIDTH).astype(BF16)
    row = lambda a: a.reshape(1, -1).astype(F32)

    rq, rk, rv, rg, q, k, v = _inproj(x, tabs, row(attn_norm_w), win, row(mla_q_norm_w), wq,
                                      row(mla_kv_norm_w), wk, wv)
    y_ret = _retention(rq, rk, rv, rg, row(ret_gn_w))
    y_mla = _mla(q, k, v)
    return _ffn(x, y_ret, y_mla, w_out.astype(BF16), row(ffn_norm_w), w_up.astype(BF16),
                conv_w.astype(F32), row(conv_b), w_down.astype(BF16), row(out_norm_w))


def kernel(x, positions, attn_norm_w, w_in, ret_gn_w, mla_q_norm_w, w_uq, mla_kv_norm_w, w_ukv,
           w_out, ffn_norm_w, w_up, conv_w, conv_b, w_down, final_norm_w):
    depth = w_in.shape[0]
    assert depth == 1, "the final RMSNorm is fused into the (single) layer's FFN kernel"
    tabs = _rope_tables(positions)
    return _layer(x, tabs, attn_norm_w[0], w_in[0], ret_gn_w[0], mla_q_norm_w[0], w_uq[0],
                  mla_kv_norm_w[0], w_ukv[0], w_out[0], ffn_norm_w[0], w_up[0], conv_w[0],
                  conv_b[0], w_down[0], final_norm_w)
```

```python
import functools

import jax
import jax.numpy as jnp
import numpy as np
from jax import lax
from jax.experimental import pallas as pl
from jax.experimental.pallas import tpu as pltpu

F32 = jnp.float32
BF16 = jnp.bfloat16

D_MODEL = 1024
RET_HEADS = 8
RET_HEAD_DIM = 64
RET_WIDTH = RET_HEADS * RET_HEAD_DIM
MLA_HEADS = 8
MLA_NOPE_DIM = 64
MLA_ROPE_DIM = 32
MLA_V_DIM = 64
MLA_Q_RANK = 256
MLA_KV_RANK = 128
MLA_WIDTH = MLA_HEADS * MLA_V_DIM
D_FF = 2816
CONV_WIDTH = 3
ROPE_BASE = 10000.0
EPS = 1e-6
LOG2_E = 1.4426950408889634

LANES = 128
SUBLANES = 8
HEAD_PAD = LANES
MLA_QK_WIDTH = MLA_HEADS * HEAD_PAD
IN_PAD_WIDTH = 4 * RET_WIDTH + MLA_Q_RANK + MLA_KV_RANK + LANES

TM_IN = 512
RET_CHUNK = 256
TQ = 1024
TK = 512
TM_FFN = 512
TF = 256
VMEM_LIMIT = 56 * 1024 * 1024


def _rms(x, w):
    return x * lax.rsqrt(jnp.mean(x * x, axis=-1, keepdims=True) + EPS) * w


def _inproj_kernel(x_ref, tab_ref, anw_ref, win_ref, qnw_ref, wq_ref,
                   kvnw_ref, wk_ref, wv_ref,
                   rq_ref, rk_ref, rv_ref, rg_ref, q_ref, k_ref, v_ref):
    tm = x_ref.shape[1]
    h = _rms(x_ref[0], anw_ref[...]).astype(BF16)

    lane = lax.broadcasted_iota(jnp.int32, (tm, LANES), 1)
    ret_first_half = (lane % RET_HEAD_DIM) < (RET_HEAD_DIM // 2)
    mla_first_half = lane < (MLA_NOPE_DIM + MLA_ROPE_DIM // 2)
    cr, sr, cm, sm = (tab_ref[0, :, g * LANES:(g + 1) * LANES] for g in range(4))

    def rope_ret(p):
        rot = jnp.where(ret_first_half, pltpu.roll(p, LANES - 32, 1), pltpu.roll(p, 32, 1))
        return p * cr + rot * sr

    def rope_mla(p):
        rot = jnp.where(mla_first_half, pltpu.roll(p, LANES - 16, 1), pltpu.roll(p, 16, 1))
        return p * cm + rot * sm

    def seg(lo, width):
        return jnp.dot(h, win_ref[:, lo:lo + width], preferred_element_type=F32)

    cq = seg(4 * RET_WIDTH, MLA_Q_RANK)
    tail = seg(4 * RET_WIDTH + MLA_Q_RANK, MLA_KV_RANK + LANES)
    pq = seg(0, RET_WIDTH)
    cqn = _rms(cq, qnw_ref[...]).astype(BF16)
    ckvn = _rms(tail[:, :MLA_KV_RANK], kvnw_ref[...]).astype(BF16)
    kpe = rope_mla(tail[:, MLA_KV_RANK:])
    pk = seg(RET_WIDTH, RET_WIDTH)
    qf = jnp.dot(cqn, wq_ref[...], preferred_element_type=F32)
    kf = jnp.dot(ckvn, wk_ref[...], preferred_element_type=F32)
    vf = jnp.dot(ckvn, wv_ref[...], preferred_element_type=F32)
    rv_ref[0] = seg(2 * RET_WIDTH, RET_WIDTH).astype(BF16)
    rg_ref[0] = seg(3 * RET_WIDTH, RET_WIDTH)
    for g in range(RET_WIDTH // LANES):
        sl = slice(g * LANES, (g + 1) * LANES)
        rq_ref[0, :, sl] = rope_ret(pq[:, sl]).astype(BF16)
        rk_ref[0, :, sl] = (rope_ret(pk[:, sl]) * (RET_HEAD_DIM ** -0.5)).astype(BF16)
    scale = (MLA_NOPE_DIM + MLA_ROPE_DIM) ** -0.5 * LOG2_E
    for hd in range(MLA_HEADS):
        sl = slice(hd * HEAD_PAD, (hd + 1) * HEAD_PAD)
        q_ref[0, :, sl] = (rope_mla(qf[:, sl]) * scale).astype(BF16)
        k_ref[0, :, sl] = (kf[:, sl] + kpe).astype(BF16)
        v_ref[0, :, sl] = jnp.where(lane < MLA_V_DIM, vf[:, sl], 1.0).astype(BF16)


def _inproj(x, tabs, anw, win, qnw, wq, kvnw, wk, wv):
    B, S, D = x.shape
    tm = TM_IN
    grid = (B, S // tm)
    tok = lambda w: pl.BlockSpec((1, tm, w), lambda b, i: (b, i, 0))
    full = lambda a: pl.BlockSpec(a.shape, lambda b, i: (0,) * a.ndim)
    outs = [
        jax.ShapeDtypeStruct((B, S, RET_WIDTH), BF16),
        jax.ShapeDtypeStruct((B, S, RET_WIDTH), BF16),
        jax.ShapeDtypeStruct((B, S, RET_WIDTH), BF16),
        jax.ShapeDtypeStruct((B, S, RET_WIDTH), F32),
        jax.ShapeDtypeStruct((B, S, MLA_QK_WIDTH), BF16),
        jax.ShapeDtypeStruct((B, S, MLA_QK_WIDTH), BF16),
        jax.ShapeDtypeStruct((B, S, MLA_QK_WIDTH), BF16),
    ]
    return pl.pallas_call(
        _inproj_kernel,
        grid=grid,
        in_specs=[tok(D), tok(4 * LANES)] + [full(a) for a in (anw, win, qnw, wq, kvnw, wk, wv)],
        out_specs=[tok(o.shape[-1]) for o in outs],
        out_shape=outs,
        compiler_params=pltpu.CompilerParams(
            dimension_semantics=("parallel", "parallel"), vmem_limit_bytes=VMEM_LIMIT),
        name="inproj",
    )(x, tabs, anw, win, qnw, wq, kvnw, wk, wv)


def _retention_kernel(q_ref, k_ref, v_ref, g_ref, dm_ref, xi_ref, zeta_ref, sdec_ref, smask_ref,
                      gnw_ref, o_ref, r_ref):
    C = q_ref.shape[1]

    @pl.when(pl.program_id(1) == 0)
    def _():
        r_ref[...] = jnp.zeros_like(r_ref)

    lane = lax.broadcasted_iota(jnp.int32, (C, LANES), 1)
    first = lane < RET_HEAD_DIM
    inv_n = 1.0 / RET_HEAD_DIM
    for p in range(RET_HEADS // 2):
        sl = slice(p * LANES, (p + 1) * LANES)
        q = q_ref[0, :, sl]
        k = k_ref[0, :, sl]
        v = v_ref[0, :, sl]
        zero = jnp.zeros_like(q)
        inner = []
        for hh, qm in enumerate((jnp.where(first, q, zero), jnp.where(first, zero, q))):
            s = lax.dot_general(qm, k, (((1,), (1,)), ((), ())), preferred_element_type=F32)
            s = (s * dm_ref[2 * p + hh]).astype(BF16)
            inner.append(jnp.dot(s, v, preferred_element_type=F32))
        o = jnp.where(first, inner[0], inner[1])
        r_prev = r_ref[p]
        o = o + jnp.dot(q, r_prev.astype(BF16), preferred_element_type=F32) * xi_ref[p]
        kz = (k.astype(F32) * zeta_ref[p]).T.astype(BF16)
        upd = jnp.dot(kz, v, preferred_element_type=F32)
        r_ref[p] = sdec_ref[p] * r_prev + smask_ref[...] * upd

        s_all = jnp.sum(o, axis=-1, keepdims=True)
        s_1 = jnp.sum(jnp.where(first, o, 0.0), axis=-1, keepdims=True)
        mu = jnp.where(first, s_1, s_all - s_1) * inv_n
        d = o - mu
        d2 = d * d
        v_all = jnp.sum(d2, axis=-1, keepdims=True)
        v_1 = jnp.sum(jnp.where(first, d2, 0.0), axis=-1, keepdims=True)
        var = jnp.where(first, v_1, v_all - v_1) * inv_n
        y = d * lax.rsqrt(var + EPS) * gnw_ref[:, sl]
        g = g_ref[0, :, sl]
        o_ref[0, :, sl] = (g * jax.nn.sigmoid(g) * y).astype(BF16)


def _retention(rq, rk, rv, rg, gnw):
    B, S, W = rq.shape
    C = RET_CHUNK
    H = RET_HEADS
    log_gamma = np.log1p(-np.power(2.0, -5.0 - np.arange(H, dtype=np.float64)))
    idx = np.arange(C, dtype=np.float64)
    diff = idx[:, None] - idx[None, :]
    dm = np.where(diff >= 0, np.exp(log_gamma[:, None, None] * np.maximum(diff, 0.0)), 0.0)
    lane_head = np.arange(W) // RET_HEAD_DIM
    xi = np.exp(log_gamma[lane_head][None, :] * (idx[:, None] + 1.0))
    zeta = np.exp(log_gamma[lane_head][None, :] * (C - 1.0 - idx[:, None]))
    to_pairs = lambda a: np.ascontiguousarray(a.reshape(C, H // 2, LANES).transpose(1, 0, 2))
    row_head = np.arange(LANES) // RET_HEAD_DIM
    smask = (row_head[:, None] == row_head[None, :]).astype(np.float64)
    sdec = np.stack([np.exp(log_gamma[2 * p + row_head] * C)[:, None] * np.ones((1, LANES))
                     for p in range(H // 2)])
    consts = [jnp.asarray(a, F32) for a in (dm, to_pairs(xi), to_pairs(zeta), sdec, smask)]

    tok = lambda: pl.BlockSpec((1, C, W), lambda b, n: (b, n, 0))
    full = lambda a: pl.BlockSpec(a.shape, lambda b, n: (0,) * a.ndim)
    return pl.pallas_call(
        _retention_kernel,
        grid=(B, S // C),
        in_specs=[tok(), tok(), tok(), tok()] + [full(a) for a in consts] + [full(gnw)],
        out_specs=tok(),
        out_shape=jax.ShapeDtypeStruct((B, S, W), BF16),
        scratch_shapes=[pltpu.VMEM((H // 2, LANES, LANES), F32)],
        compiler_params=pltpu.CompilerParams(
            dimension_semantics=("parallel", "arbitrary"), vmem_limit_bytes=VMEM_LIMIT),
        name="retention",
    )(rq, rk, rv, rg, *consts, gnw)


def _mla_kernel(qi_ref, kj_ref, bs_ref, q_ref, k_ref, v_ref, bias_ref, o_ref, m_ref, acc_ref):
    t = pl.program_id(1)
    i = qi_ref[t]
    j = kj_ref[t]
    tq, tk = q_ref.shape[1], k_ref.shape[1]
    kt_per_qt = tq // tk

    @pl.when(j == 0)
    def _():
        m_ref[...] = jnp.full_like(m_ref, -jnp.inf)
        acc_ref[...] = jnp.zeros_like(acc_ref)

    def step(diagonal):
        def scores(hd):
            sl = slice(hd * HEAD_PAD, (hd + 1) * HEAD_PAD)
            s = lax.dot_general(q_ref[0, :, sl], k_ref[0, :, sl], (((1,), (1,)), ((), ())),
                                preferred_element_type=F32)
            return s + bias_ref[0] if diagonal else s

        s_next = scores(0)
        for hd in range(MLA_HEADS):
            sl = slice(hd * HEAD_PAD, (hd + 1) * HEAD_PAD)
            s = s_next
            if hd + 1 < MLA_HEADS:
                s_next = scores(hd + 1)
            m_prev = m_ref[hd]
            m_new = jnp.maximum(m_prev, jnp.max(s, axis=-1, keepdims=True))
            a = jnp.exp2(m_prev - m_new)
            pr = jnp.exp2(s - m_new[:, :1]).astype(BF16)
            m_ref[hd] = m_new
            acc_ref[hd] = a * acc_ref[hd] + jnp.dot(pr, v_ref[0, :, sl], preferred_element_type=F32)

    @pl.when(j < kt_per_qt * i)
    def _():
        step(False)

    @pl.when(j >= kt_per_qt * i)
    def _():
        step(True)

    @pl.when(j == kt_per_qt * i + kt_per_qt - 1)
    def _():
        lane = lax.broadcasted_iota(jnp.int32, (tq, LANES), 1)
        first = lane < MLA_V_DIM
        for p in range(MLA_HEADS // 2):
            a0, a1 = acc_ref[2 * p], acc_ref[2 * p + 1]
            o0 = a0 / pltpu.roll(a0, MLA_V_DIM, 1)
            o1 = a1 / pltpu.roll(a1, MLA_V_DIM, 1)
            o_ref[0, :, p * LANES:(p + 1) * LANES] = jnp.where(
                first, o0, pltpu.roll(o1, MLA_V_DIM, 1)).astype(BF16)


def _mla(q, k, v):
    B, S, _ = q.shape
    kt_per_qt = TQ // TK
    pairs = [(i, j) for i in range(S // TQ) for j in range(kt_per_qt * (i + 1))]
    qi = jnp.asarray(np.array([p[0] for p in pairs], np.int32))
    kj = jnp.asarray(np.array([p[1] for p in pairs], np.int32))
    bs = jnp.asarray(np.array([max(p[1] - kt_per_qt * p[0], 0) for p in pairs], np.int32))
    row = np.arange(TQ)[None, :, None]
    col = np.arange(TK)[None, None, :] + TK * np.arange(kt_per_qt)[:, None, None]
    bias = jnp.asarray(np.where(col <= row, 0.0, -np.inf), F32)
    grid_spec = pltpu.PrefetchScalarGridSpec(
        num_scalar_prefetch=3,
        grid=(B, len(pairs)),
        in_specs=[
            pl.BlockSpec((1, TQ, MLA_QK_WIDTH), lambda b, t, qi, kj, bs: (b, qi[t], 0)),
            pl.BlockSpec((1, TK, MLA_QK_WIDTH), lambda b, t, qi, kj, bs: (b, kj[t], 0)),
            pl.BlockSpec((1, TK, MLA_QK_WIDTH), lambda b, t, qi, kj, bs: (b, kj[t], 0)),
            pl.BlockSpec((1, TQ, TK), lambda b, t, qi, kj, bs: (bs[t], 0, 0)),
        ],
        out_specs=pl.BlockSpec((1, TQ, MLA_WIDTH), lambda b, t, qi, kj, bs: (b, qi[t], 0)),
        scratch_shapes=[
            pltpu.VMEM((MLA_HEADS, TQ, LANES), F32),
            pltpu.VMEM((MLA_HEADS, TQ, LANES), F32),
        ],
    )
    return pl.pallas_call(
        _mla_kernel,
        grid_spec=grid_spec,
        out_shape=jax.ShapeDtypeStruct((B, S, MLA_WIDTH), BF16),
        compiler_params=pltpu.CompilerParams(
            dimension_semantics=("parallel", "arbitrary"), vmem_limit_bytes=VMEM_LIMIT),
        name="mla_attention",
    )(qi, kj, bs, q, k, v, bias)


def _ffn_kernel(x_ref, yr_ref, ym_ref, wo_ref, fnw_ref, wup_ref, cw_ref, cb_ref, wdn_ref, onw_ref,
                o_ref, carry_ref, act_ref):
    tm = x_ref.shape[1]

    @pl.when(pl.program_id(1) == 0)
    def _():
        carry_ref[...] = jnp.zeros_like(carry_ref)

    x1 = (x_ref[0]
          + jnp.dot(yr_ref[0], wo_ref[:RET_WIDTH, :], preferred_element_type=F32)
          + jnp.dot(ym_ref[0], wo_ref[RET_WIDTH:, :], preferred_element_type=F32))
    h = _rms(x1, fnw_ref[...]).astype(BF16)

    first_row = lax.broadcasted_iota(jnp.int32, (SUBLANES, TF), 0) == 0

    def shift_down(v, row0):
        r = pltpu.roll(v, 1, 0)
        return jnp.concatenate([jnp.where(first_row, row0, r[:SUBLANES]), r[SUBLANES:]], axis=0)

    def causal_conv(lo):
        cols = slice(lo, lo + TF)
        u = jnp.dot(h, wup_ref[:, cols], preferred_element_type=F32)
        w0, w1, w2 = cw_ref[0:1, cols], cw_ref[1:2, cols], cw_ref[2:3, cols]
        prev = carry_ref[:, cols]
        p2, p1 = prev[SUBLANES - 2:SUBLANES - 1], prev[SUBLANES - 1:SUBLANES]
        carry_ref[:, cols] = u[tm - SUBLANES:]
        v = shift_down(w0 * u, w0 * p1) + w1 * u
        return shift_down(v, w0 * p2 + w1 * p1) + w2 * u + cb_ref[:, cols]

    for f in range(D_FF // TF):
        gate = causal_conv(f * TF)
        val = causal_conv(D_FF + f * TF)
        act_ref[:, f * TF:(f + 1) * TF] = (gate * jax.nn.sigmoid(gate) * val).astype(BF16)

    x2 = x1 + jnp.dot(act_ref[...], wdn_ref[...], preferred_element_type=F32)
    o_ref[0] = _rms(x2, onw_ref[...])


def _ffn(x, y_ret, y_mla, wo, fnw, wup, cw, cb, wdn, onw):
    B, S, D = x.shape
    tm = TM_FFN
    tok = lambda w: pl.BlockSpec((1, tm, w), lambda b, i: (b, i, 0))
    full = lambda a: pl.BlockSpec(a.shape, lambda b, i: (0,) * a.ndim, pipeline_mode=pl.Buffered(1))
    return pl.pallas_call(
        _ffn_kernel,
        grid=(B, S // tm),
        in_specs=[tok(D), tok(RET_WIDTH), tok(MLA_WIDTH)] + [full(a) for a in (wo, fnw, wup, cw, cb, wdn, onw)],
        out_specs=tok(D),
        out_shape=jax.ShapeDtypeStruct((B, S, D), F32),
        scratch_shapes=[
            pltpu.VMEM((SUBLANES, 2 * D_FF), F32),
            pltpu.VMEM((tm, D_FF), BF16),
        ],
        compiler_params=pltpu.CompilerParams(
            dimension_semantics=("parallel", "arbitrary"), vmem_limit_bytes=VMEM_LIMIT),
        name="outproj_ffn",
    )(x, y_ret, y_mla, wo, fnw, wup, cw, cb, wdn, onw)


def _rope_tables(positions):
    pos = positions.astype(F32)[..., None]

    def cos_sin(d):
        inv_freq = ROPE_BASE ** (-jnp.arange(0, d, 2, dtype=F32) / d)
        ang = pos * inv_freq
        return jnp.cos(ang), jnp.sin(ang)

    cos_r, sin_r = cos_sin(RET_HEAD_DIM)
    cos_m, sin_m = cos_sin(MLA_ROPE_DIM)
    const = lambda v, w: jnp.full(pos.shape[:-1] + (w,), v, F32)
    pad = HEAD_PAD - MLA_NOPE_DIM - MLA_ROPE_DIM
    return jnp.concatenate(
        [cos_r] * 4 + [-sin_r, sin_r] * 2
        + [const(1.0, MLA_NOPE_DIM), cos_m, cos_m, const(1.0, pad)]
        + [const(0.0, MLA_NOPE_DIM), -sin_m, sin_m, const(0.0, pad)], axis=-1)


def _layer(x, tabs, attn_norm_w, w_in, ret_gn_w, mla_q_norm_w, w_uq, mla_kv_norm_w, w_ukv,
           w_out, ffn_norm_w, w_up, conv_w, conv_b, w_down, out_norm_w):
    D = x.shape[-1]
    qk = MLA_NOPE_DIM + MLA_ROPE_DIM
    kpe_cols = jnp.zeros((D, LANES), F32).at[:, MLA_NOPE_DIM:qk].set(w_in[:, -MLA_ROPE_DIM:])
    win = jnp.concatenate([w_in[:, :-MLA_ROPE_DIM], kpe_cols], axis=1).astype(BF16)
    wq = jnp.pad(w_uq.reshape(MLA_Q_RANK, MLA_HEADS, qk), ((0, 0), (0, 0), (0, HEAD_PAD - qk)))
    wq = wq.reshape(MLA_Q_RANK, MLA_QK_WIDTH).astype(BF16)
    wkv = w_ukv.reshape(MLA_KV_RANK, MLA_HEADS, MLA_NOPE_DIM + MLA_V_DIM)
    wk = jnp.pad(wkv[:, :, :MLA_NOPE_DIM], ((0, 0), (0, 0), (0, HEAD_PAD - MLA_NOPE_DIM)))
    wk = wk.reshape(MLA_KV_RANK, MLA_QK_WIDTH).astype(BF16)
    wv = jnp.pad(wkv[:, :, MLA_NOPE_DIM:], ((0, 0), (0, 0), (0, HEAD_PAD - MLA_V_DIM)))
    wv = wv.reshape(MLA_KV_RANK, MLA_QK_WIDTH).astype(BF16)
    row = lambda a: a.reshape(1, -1).astype(F32)

    rq, rk, rv, rg, q, k, v = _inproj(x, tabs, row(attn_norm_w), win, row(mla_q_norm_w), wq,
                                      row(mla_kv_norm_w), wk, wv)
    y_ret = _retention(rq, rk, rv, rg, row(ret_gn_w))
    y_mla = _mla(q, k, v)
    return _ffn(x, y_ret, y_mla, w_out.astype(BF16), row(ffn_norm_w), w_up.astype(BF16),
                conv_w.astype(F32), row(conv_b), w_down.astype(BF16), row(out_norm_w))


def kernel(x, positions, attn_norm_w, w_in, ret_gn_w, mla_q_norm_w, w_uq, mla_kv_norm_w, w_ukv,
           w_out, ffn_norm_w, w_up, conv_w, conv_b, w_down, final_norm_w):
    depth = w_in.shape[0]
    assert depth == 1, "the final RMSNorm is fused into the (single) layer's FFN kernel"
    tabs = _rope_tables(positions)
    return _layer(x, tabs, attn_norm_w[0], w_in[0], ret_gn_w[0], mla_q_norm_w[0], w_uq[0],
                  mla_kv_norm_w[0], w_ukv[0], w_out[0], ffn_norm_w[0], w_up[0], conv_w[0],
                  conv_b[0], w_down[0], final_norm_w)
```

```python
import jax
import jax.numpy as jnp
import numpy as np
from jax import lax
from jax.experimental import pallas as pl
from jax.experimental.pallas import tpu as pltpu

F32 = jnp.float32
BF16 = jnp.bfloat16

D_MODEL = 1024
RET_HEADS = 8
RET_HEAD_DIM = 64
RET_WIDTH = RET_HEADS * RET_HEAD_DIM
MLA_HEADS = 8
MLA_NOPE_DIM = 64
MLA_ROPE_DIM = 32
MLA_V_DIM = 64
MLA_Q_RANK = 256
MLA_KV_RANK = 128
MLA_WIDTH = MLA_HEADS * MLA_V_DIM
D_FF = 2816
CONV_WIDTH = 3
ROPE_BASE = 10000.0
EPS = 1e-6
LOG2_E = 1.4426950408889634

LANES = 128
SUBLANES = 8
HEAD_PAD = LANES
MLA_QK_WIDTH = MLA_HEADS * HEAD_PAD
IN_PAD_WIDTH = 4 * RET_WIDTH + MLA_Q_RANK + MLA_KV_RANK + LANES

TM_IN = 512
RET_CHUNK = 256
TQ = 1024
TK = 512
TM_FFN = 512
TF = 256
VMEM_LIMIT = 56 * 1024 * 1024


def _rms(x, w):
    return x * lax.rsqrt(jnp.mean(x * x, axis=-1, keepdims=True) + EPS) * w


def _inproj_kernel(x_ref, pos_ref, rot_ref, anw_ref, win_ref, qnw_ref, wq_ref,
                   kvnw_ref, wk_ref, wv_ref,
                   rq_ref, rk_ref, rv_ref, rg_ref, q_ref, k_ref, v_ref):
    tm = x_ref.shape[1]
    h = _rms(x_ref[0], anw_ref[...]).astype(BF16)

    lane = lax.broadcasted_iota(jnp.int32, (tm, LANES), 1)
    ret_first_half = (lane % RET_HEAD_DIM) < (RET_HEAD_DIM // 2)
    mla_first_half = lane < (MLA_NOPE_DIM + MLA_ROPE_DIM // 2)
    pos = pos_ref[0]
    ang_r = pos * rot_ref[0:1]
    cr, sr = jnp.cos(ang_r), jnp.sin(ang_r) * rot_ref[1:2]
    ang_m = pos * rot_ref[2:3]
    cm, sm = jnp.cos(ang_m), jnp.sin(ang_m) * rot_ref[3:4]

    def rope_ret(p):
        rot = jnp.where(ret_first_half, pltpu.roll(p, LANES - 32, 1), pltpu.roll(p, 32, 1))
        return p * cr + rot * sr

    def rope_mla(p):
        rot = jnp.where(mla_first_half, pltpu.roll(p, LANES - 16, 1), pltpu.roll(p, 16, 1))
        return p * cm + rot * sm

    def seg(lo, width):
        return jnp.dot(h, win_ref[:, lo:lo + width], preferred_element_type=F32)

    cq = seg(4 * RET_WIDTH, MLA_Q_RANK)
    tail = seg(4 * RET_WIDTH + MLA_Q_RANK, MLA_KV_RANK + LANES)
    pq = seg(0, RET_WIDTH)
    cqn = _rms(cq, qnw_ref[...]).astype(BF16)
    ckvn = _rms(tail[:, :MLA_KV_RANK], kvnw_ref[...]).astype(BF16)
    kpe = rope_mla(tail[:, MLA_KV_RANK:])
    pk = seg(RET_WIDTH, RET_WIDTH)
    qf = jnp.dot(cqn, wq_ref[...], preferred_element_type=F32)
    kf = jnp.dot(ckvn, wk_ref[...], preferred_element_type=F32)
    vf = jnp.dot(ckvn, wv_ref[...], preferred_element_type=F32)
    rv_ref[0] = seg(2 * RET_WIDTH, RET_WIDTH).astype(BF16)
    rg_ref[0] = seg(3 * RET_WIDTH, RET_WIDTH)
    for g in range(RET_WIDTH // LANES):
        sl = slice(g * LANES, (g + 1) * LANES)
        rq_ref[0, :, sl] = rope_ret(pq[:, sl]).astype(BF16)
        rk_ref[0, :, sl] = (rope_ret(pk[:, sl]) * (RET_HEAD_DIM ** -0.5)).astype(BF16)
    scale = (MLA_NOPE_DIM + MLA_ROPE_DIM) ** -0.5 * LOG2_E
    for hd in range(MLA_HEADS):
        sl = slice(hd * HEAD_PAD, (hd + 1) * HEAD_PAD)
        q_ref[0, :, sl] = (rope_mla(qf[:, sl]) * scale).astype(BF16)
        k_ref[0, :, sl] = (kf[:, sl] + kpe).astype(BF16)
        v_ref[0, :, sl] = jnp.where(lane < MLA_V_DIM, vf[:, sl], 1.0).astype(BF16)


def _inproj(x, pos, rot, anw, win, qnw, wq, kvnw, wk, wv):
    B, S, D = x.shape
    tm = TM_IN
    grid = (B, S // tm)
    tok = lambda w: pl.BlockSpec((1, tm, w), lambda b, i: (b, i, 0))
    full = lambda a: pl.BlockSpec(a.shape, lambda b, i: (0,) * a.ndim)
    outs = [
        jax.ShapeDtypeStruct((B, S, RET_WIDTH), BF16),
        jax.ShapeDtypeStruct((B, S, RET_WIDTH), BF16),
        jax.ShapeDtypeStruct((B, S, RET_WIDTH), BF16),
        jax.ShapeDtypeStruct((B, S, RET_WIDTH), F32),
        jax.ShapeDtypeStruct((B, S, MLA_QK_WIDTH), BF16),
        jax.ShapeDtypeStruct((B, S, MLA_QK_WIDTH), BF16),
        jax.ShapeDtypeStruct((B, S, MLA_QK_WIDTH), BF16),
    ]
    return pl.pallas_call(
        _inproj_kernel,
        grid=grid,
        in_specs=[tok(D), tok(LANES)] + [full(a) for a in (rot, anw, win, qnw, wq, kvnw, wk, wv)],
        out_specs=[tok(o.shape[-1]) for o in outs],
        out_shape=outs,
        compiler_params=pltpu.CompilerParams(
            dimension_semantics=("parallel", "parallel"), vmem_limit_bytes=VMEM_LIMIT),
        name="inproj",
    )(x, pos, rot, anw, win, qnw, wq, kvnw, wk, wv)


def _retention_kernel(q_ref, k_ref, v_ref, g_ref, dm_ref, xi_ref, zeta_ref, sdec_ref, smask_ref,
                      gnw_ref, o_ref, r_ref):
    C = q_ref.shape[1]

    @pl.when(pl.program_id(1) == 0)
    def _():
        r_ref[...] = jnp.zeros_like(r_ref)

    lane = lax.broadcasted_iota(jnp.int32, (C, LANES), 1)
    first = lane < RET_HEAD_DIM
    inv_n = 1.0 / RET_HEAD_DIM
    for p in range(RET_HEADS // 2):
        sl = slice(p * LANES, (p + 1) * LANES)
        q = q_ref[0, :, sl]
        k = k_ref[0, :, sl]
        v = v_ref[0, :, sl]
        zero = jnp.zeros_like(q)
        inner = []
        for hh, qm in enumerate((jnp.where(first, q, zero), jnp.where(first, zero, q))):
            s = lax.dot_general(qm, k, (((1,), (1,)), ((), ())), preferred_element_type=F32)
            s = (s * dm_ref[2 * p + hh]).astype(BF16)
            inner.append(jnp.dot(s, v, preferred_element_type=F32))
        o = jnp.where(first, inner[0], inner[1])
        r_prev = r_ref[p]
        o = o + jnp.dot(q, r_prev.astype(BF16), preferred_element_type=F32) * xi_ref[p]
        kz = (k.astype(F32) * zeta_ref[p]).T.astype(BF16)
        upd = jnp.dot(kz, v, preferred_element_type=F32)
        r_ref[p] = sdec_ref[p] * r_prev + smask_ref[...] * upd

        s_all = jnp.sum(o, axis=-1, keepdims=True)
        s_1 = jnp.sum(jnp.where(first, o, 0.0), axis=-1, keepdims=True)
        mu = jnp.where(first, s_1, s_all - s_1) * inv_n
        d = o - mu
        d2 = d * d
        v_all = jnp.sum(d2, axis=-1, keepdims=True)
        v_1 = jnp.sum(jnp.where(first, d2, 0.0), axis=-1, keepdims=True)
        var = jnp.where(first, v_1, v_all - v_1) * inv_n
        y = d * lax.rsqrt(var + EPS) * gnw_ref[:, sl]
        g = g_ref[0, :, sl]
        o_ref[0, :, sl] = (g * jax.nn.sigmoid(g) * y).astype(BF16)


def _retention(rq, rk, rv, rg, gnw):
    B, S, W = rq.shape
    C = RET_CHUNK
    H = RET_HEADS
    log_gamma = np.log1p(-np.power(2.0, -5.0 - np.arange(H, dtype=np.float64)))
    idx = np.arange(C, dtype=np.float64)
    diff = idx[:, None] - idx[None, :]
    dm = np.where(diff >= 0, np.exp(log_gamma[:, None, None] * np.maximum(diff, 0.0)), 0.0)
    lane_head = np.arange(W) // RET_HEAD_DIM
    xi = np.exp(log_gamma[lane_head][None, :] * (idx[:, None] + 1.0))
    zeta = np.exp(log_gamma[lane_head][None, :] * (C - 1.0 - idx[:, None]))
    to_pairs = lambda a: np.ascontiguousarray(a.reshape(C, H // 2, LANES).transpose(1, 0, 2))
    row_head = np.arange(LANES) // RET_HEAD_DIM
    smask = (row_head[:, None] == row_head[None, :]).astype(np.float64)
    sdec = np.stack([np.exp(log_gamma[2 * p + row_head] * C)[:, None] * np.ones((1, LANES))
                     for p in range(H // 2)])
    consts = [jnp.asarray(a, F32) for a in (dm, to_pairs(xi), to_pairs(zeta), sdec, smask)]

    tok = lambda: pl.BlockSpec((1, C, W), lambda b, n: (b, n, 0))
    full = lambda a: pl.BlockSpec(a.shape, lambda b, n: (0,) * a.ndim)
    return pl.pallas_call(
        _retention_kernel,
        grid=(B, S // C),
        in_specs=[tok(), tok(), tok(), tok()] + [full(a) for a in consts] + [full(gnw)],
        out_specs=tok(),
        out_shape=jax.ShapeDtypeStruct((B, S, W), BF16),
        scratch_shapes=[pltpu.VMEM((H // 2, LANES, LANES), F32)],
        compiler_params=pltpu.CompilerParams(
            dimension_semantics=("parallel", "arbitrary"), vmem_limit_bytes=VMEM_LIMIT),
        name="retention",
    )(rq, rk, rv, rg, *consts, gnw)


def _mla_kernel(qi_ref, kj_ref, bs_ref, q_ref, k_ref, v_ref, bias_ref, o_ref, m_ref, acc_ref):
    t = pl.program_id(1)
    i = qi_ref[t]
    j = kj_ref[t]
    tq, tk = q_ref.shape[1], k_ref.shape[1]
    kt_per_qt = tq // tk

    @pl.when(j == 0)
    def _():
        m_ref[...] = jnp.full_like(m_ref, -jnp.inf)
        acc_ref[...] = jnp.zeros_like(acc_ref)

    def step(diagonal):
        def scores(hd):
            sl = slice(hd * HEAD_PAD, (hd + 1) * HEAD_PAD)
            s = lax.dot_general(q_ref[0, :, sl], k_ref[0, :, sl], (((1,), (1,)), ((), ())),
                                preferred_element_type=F32)
            return s + bias_ref[0] if diagonal else s

        s_next = scores(0)
        for hd in range(MLA_HEADS):
            sl = slice(hd * HEAD_PAD, (hd + 1) * HEAD_PAD)
            s = s_next
            if hd + 1 < MLA_HEADS:
                s_next = scores(hd + 1)
            m_prev = m_ref[hd]
            m_new = jnp.maximum(m_prev, jnp.max(s, axis=-1, keepdims=True))
            a = jnp.exp2(m_prev - m_new)
            pr = jnp.exp2(s - m_new[:, :1]).astype(BF16)
            m_ref[hd] = m_new
            acc_ref[hd] = a * acc_ref[hd] + jnp.dot(pr, v_ref[0, :, sl], preferred_element_type=F32)

    @pl.when(j < kt_per_qt * i)
    def _():
        step(False)

    @pl.when(j >= kt_per_qt * i)
    def _():
        step(True)

    @pl.when(j == kt_per_qt * i + kt_per_qt - 1)
    def _():
        lane = lax.broadcasted_iota(jnp.int32, (tq, LANES), 1)
        first = lane < MLA_V_DIM
        for p in range(MLA_HEADS // 2):
            a0, a1 = acc_ref[2 * p], acc_ref[2 * p + 1]
            o0 = a0 / pltpu.roll(a0, MLA_V_DIM, 1)
            o1 = a1 / pltpu.roll(a1, MLA_V_DIM, 1)
            o_ref[0, :, p * LANES:(p + 1) * LANES] = jnp.where(
                first, o0, pltpu.roll(o1, MLA_V_DIM, 1)).astype(BF16)


def _mla(q, k, v):
    B, S, _ = q.shape
    kt_per_qt = TQ // TK
    pairs = [(i, j) for i in range(S // TQ) for j in range(kt_per_qt * (i + 1))]
    qi = jnp.asarray(np.array([p[0] for p in pairs], np.int32))
    kj = jnp.asarray(np.array([p[1] for p in pairs], np.int32))
    bs = jnp.asarray(np.array([max(p[1] - kt_per_qt * p[0], 0) for p in pairs], np.int32))
    row = np.arange(TQ)[None, :, None]
    col = np.arange(TK)[None, None, :] + TK * np.arange(kt_per_qt)[:, None, None]
    bias = jnp.asarray(np.where(col <= row, 0.0, -np.inf), F32)
    grid_spec = pltpu.PrefetchScalarGridSpec(
        num_scalar_prefetch=3,
        grid=(B, len(pairs)),
        in_specs=[
            pl.BlockSpec((1, TQ, MLA_QK_WIDTH), lambda b, t, qi, kj, bs: (b, qi[t], 0)),
            pl.BlockSpec((1, TK, MLA_QK_WIDTH), lambda b, t, qi, kj, bs: (b, kj[t], 0)),
            pl.BlockSpec((1, TK, MLA_QK_WIDTH), lambda b, t, qi, kj, bs: (b, kj[t], 0)),
            pl.BlockSpec((1, TQ, TK), lambda b, t, qi, kj, bs: (bs[t], 0, 0)),
        ],
        out_specs=pl.BlockSpec((1, TQ, MLA_WIDTH), lambda b, t, qi, kj, bs: (b, qi[t], 0)),
        scratch_shapes=[
            pltpu.VMEM((MLA_HEADS, TQ, LANES), F32),
            pltpu.VMEM((MLA_HEADS, TQ, LANES), F32),
        ],
    )
    return pl.pallas_call(
        _mla_kernel,
        grid_spec=grid_spec,
        out_shape=jax.ShapeDtypeStruct((B, S, MLA_WIDTH), BF16),
        compiler_params=pltpu.CompilerParams(
            dimension_semantics=("parallel", "arbitrary"), vmem_limit_bytes=VMEM_LIMIT),
        name="mla_attention",
    )(qi, kj, bs, q, k, v, bias)


def _ffn_kernel(x_ref, yr_ref, ym_ref, wo_ref, fnw_ref, wup_ref, cw_ref, cb_ref, wdn_ref, onw_ref,
                o_ref, carry_ref, act_ref):
    tm = x_ref.shape[1]

    @pl.when(pl.program_id(1) == 0)
    def _():
        carry_ref[...] = jnp.zeros_like(carry_ref)

    x1 = (x_ref[0]
          + jnp.dot(yr_ref[0], wo_ref[:RET_WIDTH, :], preferred_element_type=F32)
          + jnp.dot(ym_ref[0], wo_ref[RET_WIDTH:, :], preferred_element_type=F32))
    h = _rms(x1, fnw_ref[...]).astype(BF16)

    first_row = lax.broadcasted_iota(jnp.int32, (SUBLANES, TF), 0) == 0

    def shift_down(v, row0):
        r = pltpu.roll(v, 1, 0)
        return jnp.concatenate([jnp.where(first_row, row0, r[:SUBLANES]), r[SUBLANES:]], axis=0)

    def causal_conv(lo):
        cols = slice(lo, lo + TF)
        u = jnp.dot(h, wup_ref[:, cols], preferred_element_type=F32)
        w0, w1, w2 = cw_ref[0:1, cols], cw_ref[1:2, cols], cw_ref[2:3, cols]
        prev = carry_ref[:, cols]
        p2, p1 = prev[SUBLANES - 2:SUBLANES - 1], prev[SUBLANES - 1:SUBLANES]
        carry_ref[:, cols] = u[tm - SUBLANES:]
        v = shift_down(w0 * u, w0 * p1) + w1 * u
        return shift_down(v, w0 * p2 + w1 * p1) + w2 * u + cb_ref[:, cols]

    for f in range(D_FF // TF):
        gate = causal_conv(f * TF)
        val = causal_conv(D_FF + f * TF)
        act_ref[:, f * TF:(f + 1) * TF] = (gate * jax.nn.sigmoid(gate) * val).astype(BF16)

    x2 = x1 + jnp.dot(act_ref[...], wdn_ref[...], preferred_element_type=F32)
    o_ref[0] = _rms(x2, onw_ref[...])


def _ffn(x, y_ret, y_mla, wo, fnw, wup, cw, cb, wdn, onw):
    B, S, D = x.shape
    tm = TM_FFN
    tok = lambda w: pl.BlockSpec((1, tm, w), lambda b, i: (b, i, 0))
    full = lambda a: pl.BlockSpec(a.shape, lambda b, i: (0,) * a.ndim, pipeline_mode=pl.Buffered(1))
    return pl.pallas_call(
        _ffn_kernel,
        grid=(B, S // tm),
        in_specs=[tok(D), tok(RET_WIDTH), tok(MLA_WIDTH)] + [full(a) for a in (wo, fnw, wup, cw, cb, wdn, onw)],
        out_specs=tok(D),
        out_shape=jax.ShapeDtypeStruct((B, S, D), F32),
        scratch_shapes=[
            pltpu.VMEM((SUBLANES, 2 * D_FF), F32),
            pltpu.VMEM((tm, D_FF), BF16),
        ],
        compiler_params=pltpu.CompilerParams(
            dimension_semantics=("parallel", "arbitrary"), vmem_limit_bytes=VMEM_LIMIT),
        name="outproj_ffn",
    )(x, y_ret, y_mla, wo, fnw, wup, cw, cb, wdn, onw)


def _rotary_lanes():
    lane = np.arange(LANES)

    def rows(d, active, first_half, freq_idx):
        inv_freq = ROPE_BASE ** (-jnp.arange(0, d, 2, dtype=F32) / d)
        freq = jnp.where(active, inv_freq[freq_idx], 0.0)
        return freq, jnp.asarray(np.where(first_half, -1.0, 1.0), F32)

    half_r = RET_HEAD_DIM // 2
    fr, gr = rows(RET_HEAD_DIM, np.ones(LANES, bool), (lane % RET_HEAD_DIM) < half_r, lane % half_r)
    half_m = MLA_ROPE_DIM // 2
    in_rope = (lane >= MLA_NOPE_DIM) & (lane < MLA_NOPE_DIM + MLA_ROPE_DIM)
    fm, gm = rows(MLA_ROPE_DIM, in_rope, lane < MLA_NOPE_DIM + half_m, (lane - MLA_NOPE_DIM) % half_m)
    return jnp.stack([fr, gr, fm, gm]).astype(F32)


def _layer(x, pos, rot, attn_norm_w, w_in, ret_gn_w, mla_q_norm_w, w_uq, mla_kv_norm_w, w_ukv,
           w_out, ffn_norm_w, w_up, conv_w, conv_b, w_down, out_norm_w):
    D = x.shape[-1]
    qk = MLA_NOPE_DIM + MLA_ROPE_DIM
    kpe_cols = jnp.zeros((D, LANES), F32).at[:, MLA_NOPE_DIM:qk].set(w_in[:, -MLA_ROPE_DIM:])
    win = jnp.concatenate([w_in[:, :-MLA_ROPE_DIM], kpe_cols], axis=1).astype(BF16)
    wq = jnp.pad(w_uq.reshape(MLA_Q_RANK, MLA_HEADS, qk), ((0, 0), (0, 0), (0, HEAD_PAD - qk)))
    wq = wq.reshape(MLA_Q_RANK, MLA_QK_WIDTH).astype(BF16)
    wkv = w_ukv.reshape(MLA_KV_RANK, MLA_HEADS, MLA_NOPE_DIM + MLA_V_DIM)
    wk = jnp.pad(wkv[:, :, :MLA_NOPE_DIM], ((0, 0), (0, 0), (0, HEAD_PAD - MLA_NOPE_DIM)))
    wk = wk.reshape(MLA_KV_RANK, MLA_QK_WIDTH).astype(BF16)
    wv = jnp.pad(wkv[:, :, MLA_NOPE_DIM:], ((0, 0), (0, 0), (0, HEAD_PAD - MLA_V_DIM)))
    wv = wv.reshape(MLA_KV_RANK, MLA_QK_WIDTH).astype(BF16)
    row = lambda a: a.reshape(1, -1).astype(F32)

    rq, rk, rv, rg, q, k, v = _inproj(x, pos, rot, row(attn_norm_w), win, row(mla_q_norm_w), wq,
                                      row(mla_kv_norm_w), wk, wv)
    y_ret = _retention(rq, rk, rv, rg, row(ret_gn_w))
    y_mla = _mla(q, k, v)
    return _ffn(x, y_ret, y_mla, w_out.astype(BF16), row(ffn_norm_w), w_up.astype(BF16),
                conv_w.astype(F32), row(conv_b), w_down.astype(BF16), row(out_norm_w))


def kernel(x, positions, attn_norm_w, w_in, ret_gn_w, mla_q_norm_w, w_uq, mla_kv_norm_w, w_ukv,
           w_out, ffn_norm_w, w_up, conv_w, conv_b, w_down, final_norm_w):
    depth = w_in.shape[0]
    assert depth == 1, "the final RMSNorm is fused into the (single) layer's FFN kernel"
    pos = jnp.broadcast_to(positions.astype(F32)[..., None], positions.shape + (LANES,))
    return _layer(x, pos, _rotary_lanes(), attn_norm_w[0], w_in[0], ret_gn_w[0], mla_q_norm_w[0], w_uq[0],
                  mla_kv_norm_w[0], w_ukv[0], w_out[0], ffn_norm_w[0], w_up[0], conv_w[0],
                  conv_b[0], w_down[0], final_norm_w)
```

```python
import jax
import jax.numpy as jnp
import numpy as np
from jax import lax
from jax.experimental import pallas as pl
from jax.experimental.pallas import tpu as pltpu

F32 = jnp.float32
BF16 = jnp.bfloat16

D_MODEL = 1024
RET_HEADS = 8
RET_HEAD_DIM = 64
RET_WIDTH = RET_HEADS * RET_HEAD_DIM
MLA_HEADS = 8
MLA_NOPE_DIM = 64
MLA_ROPE_DIM = 32
MLA_V_DIM = 64
MLA_Q_RANK = 256
MLA_KV_RANK = 128
MLA_WIDTH = MLA_HEADS * MLA_V_DIM
D_FF = 2816
CONV_WIDTH = 3
ROPE_BASE = 10000.0
EPS = 1e-6
LOG2_E = 1.4426950408889634

LANES = 128
SUBLANES = 8
HEAD_PAD = LANES
MLA_QK_WIDTH = MLA_HEADS * HEAD_PAD
IN_PAD_WIDTH = 4 * RET_WIDTH + MLA_Q_RANK + MLA_KV_RANK + LANES

TM_IN = 512
RET_CHUNK = 256
RET_ROWS = 1024
TQ = 1024
TK = 1024
TK_SUB = 512
TM_FFN = 512
TF = 256
VMEM_LIMIT = 56 * 1024 * 1024


def _rms(x, w):
    return x * lax.rsqrt(jnp.mean(x * x, axis=-1, keepdims=True) + EPS) * w


def _inproj_kernel(x_ref, pos_ref, rot_ref, anw_ref, win_ref, qnw_ref, wq_ref,
                   kvnw_ref, wk_ref, wv_ref,
                   rq_ref, rk_ref, rv_ref, rg_ref, q_ref, k_ref, v_ref):
    tm = x_ref.shape[1]
    h = _rms(x_ref[0], anw_ref[...]).astype(BF16)

    lane = lax.broadcasted_iota(jnp.int32, (tm, LANES), 1)
    ret_first_half = (lane % RET_HEAD_DIM) < (RET_HEAD_DIM // 2)
    mla_first_half = lane < (MLA_NOPE_DIM + MLA_ROPE_DIM // 2)
    pos = pos_ref[0]
    ang_r = pos * rot_ref[0:1]
    cr, sr = jnp.cos(ang_r), jnp.sin(ang_r) * rot_ref[1:2]
    ang_m = pos * rot_ref[2:3]
    cm, sm = jnp.cos(ang_m), jnp.sin(ang_m) * rot_ref[3:4]

    def rope_ret(p):
        rot = jnp.where(ret_first_half, pltpu.roll(p, LANES - 32, 1), pltpu.roll(p, 32, 1))
        return p * cr + rot * sr

    def rope_mla(p):
        rot = jnp.where(mla_first_half, pltpu.roll(p, LANES - 16, 1), pltpu.roll(p, 16, 1))
        return p * cm + rot * sm

    def seg(lo, width):
        return jnp.dot(h, win_ref[:, lo:lo + width], preferred_element_type=F32)

    cq = seg(4 * RET_WIDTH, MLA_Q_RANK)
    tail = seg(4 * RET_WIDTH + MLA_Q_RANK, MLA_KV_RANK + LANES)
    pq = seg(0, RET_WIDTH)
    cqn = _rms(cq, qnw_ref[...]).astype(BF16)
    ckvn = _rms(tail[:, :MLA_KV_RANK], kvnw_ref[...]).astype(BF16)
    kpe = rope_mla(tail[:, MLA_KV_RANK:])
    pk = seg(RET_WIDTH, RET_WIDTH)
    qf = jnp.dot(cqn, wq_ref[...], preferred_element_type=F32)
    kf = jnp.dot(ckvn, wk_ref[...], preferred_element_type=F32)
    vf = jnp.dot(ckvn, wv_ref[...], preferred_element_type=F32)
    rv_ref[0] = seg(2 * RET_WIDTH, RET_WIDTH).astype(BF16)
    rg_ref[0] = seg(3 * RET_WIDTH, RET_WIDTH)
    for g in range(RET_WIDTH // LANES):
        sl = slice(g * LANES, (g + 1) * LANES)
        rq_ref[0, :, sl] = rope_ret(pq[:, sl]).astype(BF16)
        rk_ref[0, :, sl] = (rope_ret(pk[:, sl]) * (RET_HEAD_DIM ** -0.5)).astype(BF16)
    scale = (MLA_NOPE_DIM + MLA_ROPE_DIM) ** -0.5 * LOG2_E
    for hd in range(MLA_HEADS):
        sl = slice(hd * HEAD_PAD, (hd + 1) * HEAD_PAD)
        q_ref[0, :, sl] = (rope_mla(qf[:, sl]) * scale).astype(BF16)
        k_ref[0, :, sl] = (kf[:, sl] + kpe).astype(BF16)
        v_ref[0, :, sl] = jnp.where(lane < MLA_V_DIM, vf[:, sl], 1.0).astype(BF16)


def _inproj(x, pos, rot, anw, win, qnw, wq, kvnw, wk, wv):
    B, S, D = x.shape
    tm = TM_IN
    grid = (B, S // tm)
    tok = lambda w: pl.BlockSpec((1, tm, w), lambda b, i: (b, i, 0))
    full = lambda a: pl.BlockSpec(a.shape, lambda b, i: (0,) * a.ndim)
    outs = [
        jax.ShapeDtypeStruct((B, S, RET_WIDTH), BF16),
        jax.ShapeDtypeStruct((B, S, RET_WIDTH), BF16),
        jax.ShapeDtypeStruct((B, S, RET_WIDTH), BF16),
        jax.ShapeDtypeStruct((B, S, RET_WIDTH), F32),
        jax.ShapeDtypeStruct((B, S, MLA_QK_WIDTH), BF16),
        jax.ShapeDtypeStruct((B, S, MLA_QK_WIDTH), BF16),
        jax.ShapeDtypeStruct((B, S, MLA_QK_WIDTH), BF16),
    ]
    return pl.pallas_call(
        _inproj_kernel,
        grid=grid,
        in_specs=[tok(D), tok(LANES)] + [full(a) for a in (rot, anw, win, qnw, wq, kvnw, wk, wv)],
        out_specs=[tok(o.shape[-1]) for o in outs],
        out_shape=outs,
        compiler_params=pltpu.CompilerParams(
            dimension_semantics=("parallel", "parallel"), vmem_limit_bytes=VMEM_LIMIT),
        name="inproj",
    )(x, pos, rot, anw, win, qnw, wq, kvnw, wk, wv)


def _retention_kernel(q_ref, k_ref, v_ref, g_ref, dm_ref, xi_ref, zeta_ref, sdec_ref, smask_ref,
                      gnw_ref, o_ref, r_ref):
    C = RET_CHUNK

    @pl.when(pl.program_id(1) == 0)
    def _():
        r_ref[...] = jnp.zeros_like(r_ref)

    lane = lax.broadcasted_iota(jnp.int32, (C, LANES), 1)
    first = lane < RET_HEAD_DIM
    inv_n = 1.0 / RET_HEAD_DIM
    for c, p in [(c, p) for c in range(q_ref.shape[1] // C) for p in range(RET_HEADS // 2)]:
        rows = slice(c * C, (c + 1) * C)
        sl = slice(p * LANES, (p + 1) * LANES)
        q = q_ref[0, rows, sl]
        k = k_ref[0, rows, sl]
        v = v_ref[0, rows, sl]
        zero = jnp.zeros_like(q)
        inner = []
        for hh, qm in enumerate((jnp.where(first, q, zero), jnp.where(first, zero, q))):
            s = lax.dot_general(qm, k, (((1,), (1,)), ((), ())), preferred_element_type=F32)
            s = (s * dm_ref[2 * p + hh]).astype(BF16)
            inner.append(jnp.dot(s, v, preferred_element_type=F32))
        o = jnp.where(first, inner[0], inner[1])
        r_prev = r_ref[p]
        o = o + jnp.dot(q, r_prev.astype(BF16), preferred_element_type=F32) * xi_ref[p]
        kz = (k.astype(F32) * zeta_ref[p]).T.astype(BF16)
        upd = jnp.dot(kz, v, preferred_element_type=F32)
        r_ref[p] = sdec_ref[p] * r_prev + smask_ref[...] * upd

        s_all = jnp.sum(o, axis=-1, keepdims=True)
        s_1 = jnp.sum(jnp.where(first, o, 0.0), axis=-1, keepdims=True)
        mu = jnp.where(first, s_1, s_all - s_1) * inv_n
        d = o - mu
        d2 = d * d
        v_all = jnp.sum(d2, axis=-1, keepdims=True)
        v_1 = jnp.sum(jnp.where(first, d2, 0.0), axis=-1, keepdims=True)
        var = jnp.where(first, v_1, v_all - v_1) * inv_n
        y = d * lax.rsqrt(var + EPS) * gnw_ref[:, sl]
        g = g_ref[0, rows, sl]
        o_ref[0, rows, sl] = (g * jax.nn.sigmoid(g) * y).astype(BF16)


def _retention(rq, rk, rv, rg, gnw):
    B, S, W = rq.shape
    C = RET_CHUNK
    H = RET_HEADS
    log_gamma = np.log1p(-np.power(2.0, -5.0 - np.arange(H, dtype=np.float64)))
    idx = np.arange(C, dtype=np.float64)
    diff = idx[:, None] - idx[None, :]
    dm = np.where(diff >= 0, np.exp(log_gamma[:, None, None] * np.maximum(diff, 0.0)), 0.0)
    lane_head = np.arange(W) // RET_HEAD_DIM
    xi = np.exp(log_gamma[lane_head][None, :] * (idx[:, None] + 1.0))
    zeta = np.exp(log_gamma[lane_head][None, :] * (C - 1.0 - idx[:, None]))
    to_pairs = lambda a: np.ascontiguousarray(a.reshape(C, H // 2, LANES).transpose(1, 0, 2))
    row_head = np.arange(LANES) // RET_HEAD_DIM
    smask = (row_head[:, None] == row_head[None, :]).astype(np.float64)
    sdec = np.stack([np.exp(log_gamma[2 * p + row_head] * C)[:, None] * np.ones((1, LANES))
                     for p in range(H // 2)])
    consts = [jnp.asarray(a, F32) for a in (dm, to_pairs(xi), to_pairs(zeta), sdec, smask)]

    tok = lambda: pl.BlockSpec((1, RET_ROWS, W), lambda b, n: (b, n, 0))
    full = lambda a: pl.BlockSpec(a.shape, lambda b, n: (0,) * a.ndim)
    return pl.pallas_call(
        _retention_kernel,
        grid=(B, S // RET_ROWS),
        in_specs=[tok(), tok(), tok(), tok()] + [full(a) for a in consts] + [full(gnw)],
        out_specs=tok(),
        out_shape=jax.ShapeDtypeStruct((B, S, W), BF16),
        scratch_shapes=[pltpu.VMEM((H // 2, LANES, LANES), F32)],
        compiler_params=pltpu.CompilerParams(
            dimension_semantics=("parallel", "arbitrary"), vmem_limit_bytes=VMEM_LIMIT),
        name="retention",
    )(rq, rk, rv, rg, *consts, gnw)


def _mla_kernel(qi_ref, kj_ref, q_ref, k_ref, v_ref, bias_ref, o_ref, m_ref, acc_ref):
    t = pl.program_id(1)
    i = qi_ref[t]
    j = kj_ref[t]
    tq = q_ref.shape[1]
    n_sub = k_ref.shape[1] // TK_SUB

    @pl.when(j == 0)
    def _():
        m_ref[...] = jnp.full_like(m_ref, -jnp.inf)
        acc_ref[...] = jnp.zeros_like(acc_ref)

    def step(diagonal):
        units = [(sub, hd) for sub in range(n_sub) for hd in range(MLA_HEADS)]

        def scores(unit):
            sub, hd = unit
            sl = slice(hd * HEAD_PAD, (hd + 1) * HEAD_PAD)
            keys = k_ref[0, sub * TK_SUB:(sub + 1) * TK_SUB, sl]
            s = lax.dot_general(q_ref[0, :, sl], keys, (((1,), (1,)), ((), ())),
                                preferred_element_type=F32)
            return s + bias_ref[sub] if diagonal else s

        s_next = scores(units[0])
        for n, (sub, hd) in enumerate(units):
            sl = slice(hd * HEAD_PAD, (hd + 1) * HEAD_PAD)
            s = s_next
            if n + 1 < len(units):
                s_next = scores(units[n + 1])
            m_prev = m_ref[hd]
            m_new = jnp.maximum(m_prev, jnp.max(s, axis=-1, keepdims=True))
            a = jnp.exp2(m_prev - m_new)
            pr = jnp.exp2(s - m_new[:, :1]).astype(BF16)
            m_ref[hd] = m_new
            vals = v_ref[0, sub * TK_SUB:(sub + 1) * TK_SUB, sl]
            acc_ref[hd] = a * acc_ref[hd] + jnp.dot(pr, vals, preferred_element_type=F32)

    @pl.when(j < i)
    def _():
        step(False)

    @pl.when(j == i)
    def _():
        step(True)

    @pl.when(j == i)
    def _():
        lane = lax.broadcasted_iota(jnp.int32, (tq, LANES), 1)
        first = lane < MLA_V_DIM
        for p in range(MLA_HEADS // 2):
            a0, a1 = acc_ref[2 * p], acc_ref[2 * p + 1]
            o0 = a0 / pltpu.roll(a0, MLA_V_DIM, 1)
            o1 = a1 / pltpu.roll(a1, MLA_V_DIM, 1)
            o_ref[0, :, p * LANES:(p + 1) * LANES] = jnp.where(
                first, o0, pltpu.roll(o1, MLA_V_DIM, 1)).astype(BF16)


def _mla(q, k, v):
    B, S, _ = q.shape
    assert TQ == TK and TK % TK_SUB == 0
    pairs = [(i, j) for i in range(S // TQ) for j in range(i + 1)]
    qi = jnp.asarray(np.array([p[0] for p in pairs], np.int32))
    kj = jnp.asarray(np.array([p[1] for p in pairs], np.int32))
    row = np.arange(TQ)[None, :, None]
    col = np.arange(TK_SUB)[None, None, :] + TK_SUB * np.arange(TK // TK_SUB)[:, None, None]
    bias = jnp.asarray(np.where(col <= row, 0.0, -np.inf), F32)
    grid_spec = pltpu.PrefetchScalarGridSpec(
        num_scalar_prefetch=2,
        grid=(B, len(pairs)),
        in_specs=[
            pl.BlockSpec((1, TQ, MLA_QK_WIDTH), lambda b, t, qi, kj: (b, qi[t], 0)),
            pl.BlockSpec((1, TK, MLA_QK_WIDTH), lambda b, t, qi, kj: (b, kj[t], 0)),
            pl.BlockSpec((1, TK, MLA_QK_WIDTH), lambda b, t, qi, kj: (b, kj[t], 0)),
            pl.BlockSpec(bias.shape, lambda b, t, qi, kj: (0, 0, 0)),
        ],
        out_specs=pl.BlockSpec((1, TQ, MLA_WIDTH), lambda b, t, qi, kj: (b, qi[t], 0)),
        scratch_shapes=[
            pltpu.VMEM((MLA_HEADS, TQ, LANES), F32),
            pltpu.VMEM((MLA_HEADS, TQ, LANES), F32),
        ],
    )
    return pl.pallas_call(
        _mla_kernel,
        grid_spec=grid_spec,
        out_shape=jax.ShapeDtypeStruct((B, S, MLA_WIDTH), BF16),
        compiler_params=pltpu.CompilerParams(
            dimension_semantics=("parallel", "arbitrary"), vmem_limit_bytes=VMEM_LIMIT),
        name="mla_attention",
    )(qi, kj, q, k, v, bias)


def _ffn_kernel(x_ref, yr_ref, ym_ref, wo_ref, fnw_ref, wup_ref, cw_ref, cb_ref, wdn_ref, onw_ref,
                o_ref, carry_ref, act_ref):
    tm = x_ref.shape[1]

    @pl.when(pl.program_id(1) == 0)
    def _():
        carry_ref[...] = jnp.zeros_like(carry_ref)

    x1 = (x_ref[0]
          + jnp.dot(yr_ref[0], wo_ref[:RET_WIDTH, :], preferred_element_type=F32)
          + jnp.dot(ym_ref[0], wo_ref[RET_WIDTH:, :], preferred_element_type=F32))
    h = _rms(x1, fnw_ref[...]).astype(BF16)

    first_row = lax.broadcasted_iota(jnp.int32, (SUBLANES, TF), 0) == 0

    def shift_down(v, row0):
        r = pltpu.roll(v, 1, 0)
        return jnp.concatenate([jnp.where(first_row, row0, r[:SUBLANES]), r[SUBLANES:]], axis=0)

    def causal_conv(lo):
        cols = slice(lo, lo + TF)
        u = jnp.dot(h, wup_ref[:, cols], preferred_element_type=F32)
        w0, w1, w2 = cw_ref[0:1, cols], cw_ref[1:2, cols], cw_ref[2:3, cols]
        prev = carry_ref[:, cols]
        p2, p1 = prev[SUBLANES - 2:SUBLANES - 1], prev[SUBLANES - 1:SUBLANES]
        carry_ref[:, cols] = u[tm - SUBLANES:]
        v = shift_down(w0 * u, w0 * p1) + w1 * u
        return shift_down(v, w0 * p2 + w1 * p1) + w2 * u + cb_ref[:, cols]

    for f in range(D_FF // TF):
        gate = causal_conv(f * TF)
        val = causal_conv(D_FF + f * TF)
        act_ref[:, f * TF:(f + 1) * TF] = (gate * jax.nn.sigmoid(gate) * val).astype(BF16)

    x2 = x1 + jnp.dot(act_ref[...], wdn_ref[...], preferred_element_type=F32)
    o_ref[0] = _rms(x2, onw_ref[...])


def _ffn(x, y_ret, y_mla, wo, fnw, wup, cw, cb, wdn, onw):
    B, S, D = x.shape
    tm = TM_FFN
    tok = lambda w: pl.BlockSpec((1, tm, w), lambda b, i: (b, i, 0))
    full = lambda a: pl.BlockSpec(a.shape, lambda b, i: (0,) * a.ndim, pipeline_mode=pl.Buffered(1))
    return pl.pallas_call(
        _ffn_kernel,
        grid=(B, S // tm),
        in_specs=[tok(D), tok(RET_WIDTH), tok(MLA_WIDTH)] + [full(a) for a in (wo, fnw, wup, cw, cb, wdn, onw)],
        out_specs=tok(D),
        out_shape=jax.ShapeDtypeStruct((B, S, D), F32),
        scratch_shapes=[
            pltpu.VMEM((SUBLANES, 2 * D_FF), F32),
            pltpu.VMEM((tm, D_FF), BF16),
        ],
        compiler_params=pltpu.CompilerParams(
            dimension_semantics=("parallel", "arbitrary"), vmem_limit_bytes=VMEM_LIMIT),
        name="outproj_ffn",
    )(x, y_ret, y_mla, wo, fnw, wup, cw, cb, wdn, onw)


def _rotary_lanes():
    lane = np.arange(LANES)

    def rows(d, active, first_half, freq_idx):
        inv_freq = ROPE_BASE ** (-jnp.arange(0, d, 2, dtype=F32) / d)
        freq = jnp.where(active, inv_freq[freq_idx], 0.0)
        return freq, jnp.asarray(np.where(first_half, -1.0, 1.0), F32)

    half_r = RET_HEAD_DIM // 2
    fr, gr = rows(RET_HEAD_DIM, np.ones(LANES, bool), (lane % RET_HEAD_DIM) < half_r, lane % half_r)
    half_m = MLA_ROPE_DIM // 2
    in_rope = (lane >= MLA_NOPE_DIM) & (lane < MLA_NOPE_DIM + MLA_ROPE_DIM)
    fm, gm = rows(MLA_ROPE_DIM, in_rope, lane < MLA_NOPE_DIM + half_m, (lane - MLA_NOPE_DIM) % half_m)
    return jnp.stack([fr, gr, fm, gm]).astype(F32)


def _layer(x, pos, rot, attn_norm_w, w_in, ret_gn_w, mla_q_norm_w, w_uq, mla_kv_norm_w, w_ukv,
           w_out, ffn_norm_w, w_up, conv_w, conv_b, w_down, out_norm_w):
    D = x.shape[-1]
    qk = MLA_NOPE_DIM + MLA_ROPE_DIM
    kpe_cols = jnp.zeros((D, LANES), F32).at[:, MLA_NOPE_DIM:qk].set(w_in[:, -MLA_ROPE_DIM:])
    win = jnp.concatenate([w_in[:, :-MLA_ROPE_DIM], kpe_cols], axis=1).astype(BF16)
    wq = jnp.pad(w_uq.reshape(MLA_Q_RANK, MLA_HEADS, qk), ((0, 0), (0, 0), (0, HEAD_PAD - qk)))
    wq = wq.reshape(MLA_Q_RANK, MLA_QK_WIDTH).astype(BF16)
    wkv = w_ukv.reshape(MLA_KV_RANK, MLA_HEADS, MLA_NOPE_DIM + MLA_V_DIM)
    wk = jnp.pad(wkv[:, :, :MLA_NOPE_DIM], ((0, 0), (0, 0), (0, HEAD_PAD - MLA_NOPE_DIM)))
    wk = wk.reshape(MLA_KV_RANK, MLA_QK_WIDTH).astype(BF16)
    wv = jnp.pad(wkv[:, :, MLA_NOPE_DIM:], ((0, 0), (0, 0), (0, HEAD_PAD - MLA_V_DIM)))
    wv = wv.reshape(MLA_KV_RANK, MLA_QK_WIDTH).astype(BF16)
    row = lambda a: a.reshape(1, -1).astype(F32)

    rq, rk, rv, rg, q, k, v = _inproj(x, pos, rot, row(attn_norm_w), win, row(mla_q_norm_w), wq,
                                      row(mla_kv_norm_w), wk, wv)
    y_ret = _retention(rq, rk, rv, rg, row(ret_gn_w))
    y_mla = _mla(q, k, v)
    return _ffn(x, y_ret, y_mla, w_out.astype(BF16), row(ffn_norm_w), w_up.astype(BF16),
                conv_w.astype(F32), row(conv_b), w_down.astype(BF16), row(out_norm_w))


def kernel(x, positions, attn_norm_w, w_in, ret_gn_w, mla_q_norm_w, w_uq, mla_kv_norm_w, w_ukv,
           w_out, ffn_norm_w, w_up, conv_w, conv_b, w_down, final_norm_w):
    depth = w_in.shape[0]
    assert depth == 1, "the final RMSNorm is fused into the (single) layer's FFN kernel"
    pos = jnp.broadcast_to(positions.astype(F32)[..., None], positions.shape + (LANES,))
    return _layer(x, pos, _rotary_lanes(), attn_norm_w[0], w_in[0], ret_gn_w[0], mla_q_norm_w[0], w_uq[0],
                  mla_kv_norm_w[0], w_ukv[0], w_out[0], ffn_norm_w[0], w_up[0], conv_w[0],
                  conv_b[0], w_down[0], final_norm_w)
```

```python
import jax
import jax.numpy as jnp
import numpy as np
from jax import lax
from jax.experimental import pallas as pl
from jax.experimental.pallas import tpu as pltpu

F32 = jnp.float32
BF16 = jnp.bfloat16

D_MODEL = 1024
RET_HEADS = 8
RET_HEAD_DIM = 64
RET_WIDTH = RET_HEADS * RET_HEAD_DIM
MLA_HEADS = 8
MLA_NOPE_DIM = 64
MLA_ROPE_DIM = 32
MLA_V_DIM = 64
MLA_Q_RANK = 256
MLA_KV_RANK = 128
MLA_WIDTH = MLA_HEADS * MLA_V_DIM
D_FF = 2816
CONV_WIDTH = 3
ROPE_BASE = 10000.0
EPS = 1e-6
LOG2_E = 1.4426950408889634

LANES = 128
SUBLANES = 8
HEAD_PAD = LANES
MLA_QK_WIDTH = MLA_HEADS * HEAD_PAD
IN_PAD_WIDTH = 4 * RET_WIDTH + MLA_Q_RANK + MLA_KV_RANK + LANES

TM_IN = 512
RET_CHUNK = 256
RET_ROWS = 1024
TQ = 1024
TK = 1024
TK_SUB = 512
TM_FFN = 512
TF = 256
VMEM_LIMIT = 56 * 1024 * 1024


def _rms(x, w):
    return x * lax.rsqrt(jnp.mean(x * x, axis=-1, keepdims=True) + EPS) * w


def _inproj_kernel(x_ref, pos_ref, rot_ref, anw_ref, win_ref, qnw_ref, wq_ref,
                   kvnw_ref, wk_ref, wvt_ref,
                   rq_ref, rk_ref, rv_ref, rg_ref, q_ref, k_ref, vt_ref):
    tm = x_ref.shape[1]
    h = _rms(x_ref[0], anw_ref[...]).astype(BF16)

    lane = lax.broadcasted_iota(jnp.int32, (tm, LANES), 1)
    ret_first_half = (lane % RET_HEAD_DIM) < (RET_HEAD_DIM // 2)
    mla_first_half = lane < (MLA_NOPE_DIM + MLA_ROPE_DIM // 2)
    pos = pos_ref[0]
    ang_r = pos * rot_ref[0:1]
    cr, sr = jnp.cos(ang_r), jnp.sin(ang_r) * rot_ref[1:2]
    ang_m = pos * rot_ref[2:3]
    cm, sm = jnp.cos(ang_m), jnp.sin(ang_m) * rot_ref[3:4]

    def rope_ret(p):
        rot = jnp.where(ret_first_half, pltpu.roll(p, LANES - 32, 1), pltpu.roll(p, 32, 1))
        return p * cr + rot * sr

    def rope_mla(p):
        rot = jnp.where(mla_first_half, pltpu.roll(p, LANES - 16, 1), pltpu.roll(p, 16, 1))
        return p * cm + rot * sm

    def seg(lo, width):
        return jnp.dot(h, win_ref[:, lo:lo + width], preferred_element_type=F32)

    cq = seg(4 * RET_WIDTH, MLA_Q_RANK)
    tail = seg(4 * RET_WIDTH + MLA_Q_RANK, MLA_KV_RANK + LANES)
    pq = seg(0, RET_WIDTH)
    cqn = _rms(cq, qnw_ref[...]).astype(BF16)
    ckv_norm = _rms(tail[:, :MLA_KV_RANK], kvnw_ref[...])
    ckvn = ckv_norm.astype(BF16)
    ckvn_t = ckv_norm.T.astype(BF16)
    kpe = rope_mla(tail[:, MLA_KV_RANK:])
    pk = seg(RET_WIDTH, RET_WIDTH)
    qf = jnp.dot(cqn, wq_ref[...], preferred_element_type=F32)
    kf = jnp.dot(ckvn, wk_ref[...], preferred_element_type=F32)
    vt = jnp.dot(wvt_ref[...], ckvn_t, preferred_element_type=F32)
    rv_ref[0] = seg(2 * RET_WIDTH, RET_WIDTH).astype(BF16)
    rg_ref[0] = seg(3 * RET_WIDTH, RET_WIDTH)
    for g in range(RET_WIDTH // LANES):
        sl = slice(g * LANES, (g + 1) * LANES)
        rq_ref[0, :, sl] = rope_ret(pq[:, sl]).astype(BF16)
        rk_ref[0, :, sl] = (rope_ret(pk[:, sl]) * (RET_HEAD_DIM ** -0.5)).astype(BF16)
    scale = (MLA_NOPE_DIM + MLA_ROPE_DIM) ** -0.5 * LOG2_E
    for hd in range(MLA_HEADS):
        sl = slice(hd * HEAD_PAD, (hd + 1) * HEAD_PAD)
        q_ref[0, :, sl] = (rope_mla(qf[:, sl]) * scale).astype(BF16)
        k_ref[0, :, sl] = (kf[:, sl] + kpe).astype(BF16)
        vt_ref[0, hd, :MLA_V_DIM, :] = vt[hd * MLA_V_DIM:(hd + 1) * MLA_V_DIM].astype(BF16)
        vt_ref[0, hd, MLA_V_DIM:, :] = jnp.ones((HEAD_PAD - MLA_V_DIM, tm), BF16)


def _inproj(x, pos, rot, anw, win, qnw, wq, kvnw, wk, wv):
    B, S, D = x.shape
    tm = TM_IN
    grid = (B, S // tm)
    tok = lambda w: pl.BlockSpec((1, tm, w), lambda b, i: (b, i, 0))
    full = lambda a: pl.BlockSpec(a.shape, lambda b, i: (0,) * a.ndim)
    outs = [
        jax.ShapeDtypeStruct((B, S, RET_WIDTH), BF16),
        jax.ShapeDtypeStruct((B, S, RET_WIDTH), BF16),
        jax.ShapeDtypeStruct((B, S, RET_WIDTH), BF16),
        jax.ShapeDtypeStruct((B, S, RET_WIDTH), F32),
        jax.ShapeDtypeStruct((B, S, MLA_QK_WIDTH), BF16),
        jax.ShapeDtypeStruct((B, S, MLA_QK_WIDTH), BF16),
        jax.ShapeDtypeStruct((B, MLA_HEADS, HEAD_PAD, S), BF16),
    ]
    vt_spec = pl.BlockSpec((1, MLA_HEADS, HEAD_PAD, tm), lambda b, i: (b, 0, 0, i))
    return pl.pallas_call(
        _inproj_kernel,
        grid=grid,
        in_specs=[tok(D), tok(LANES)] + [full(a) for a in (rot, anw, win, qnw, wq, kvnw, wk, wv)],
        out_specs=[tok(o.shape[-1]) for o in outs[:-1]] + [vt_spec],
        out_shape=outs,
        compiler_params=pltpu.CompilerParams(
            dimension_semantics=("parallel", "parallel"), vmem_limit_bytes=VMEM_LIMIT),
        name="inproj",
    )(x, pos, rot, anw, win, qnw, wq, kvnw, wk, wv)


def _retention_kernel(q_ref, k_ref, v_ref, g_ref, dm_ref, xi_ref, zeta_ref, sdec_ref, smask_ref,
                      gnw_ref, o_ref, r_ref):
    C = RET_CHUNK

    @pl.when(pl.program_id(1) == 0)
    def _():
        r_ref[...] = jnp.zeros_like(r_ref)

    lane = lax.broadcasted_iota(jnp.int32, (C, LANES), 1)
    first = lane < RET_HEAD_DIM
    inv_n = 1.0 / RET_HEAD_DIM
    for c, p in [(c, p) for c in range(q_ref.shape[1] // C) for p in range(RET_HEADS // 2)]:
        rows = slice(c * C, (c + 1) * C)
        sl = slice(p * LANES, (p + 1) * LANES)
        q = q_ref[0, rows, sl]
        k = k_ref[0, rows, sl]
        v = v_ref[0, rows, sl]
        zero = jnp.zeros_like(q)
        inner = []
        for hh, qm in enumerate((jnp.where(first, q, zero), jnp.where(first, zero, q))):
            s = lax.dot_general(qm, k, (((1,), (1,)), ((), ())), preferred_element_type=F32)
            s = (s * dm_ref[2 * p + hh]).astype(BF16)
            inner.append(jnp.dot(s, v, preferred_element_type=F32))
        o = jnp.where(first, inner[0], inner[1])
        r_prev = r_ref[p]
        o = o + jnp.dot(q, r_prev.astype(BF16), preferred_element_type=F32) * xi_ref[p]
        kz = (k.astype(F32) * zeta_ref[p]).T.astype(BF16)
        upd = jnp.dot(kz, v, preferred_element_type=F32)
        r_ref[p] = sdec_ref[p] * r_prev + smask_ref[...] * upd

        s_all = jnp.sum(o, axis=-1, keepdims=True)
        s_1 = jnp.sum(jnp.where(first, o, 0.0), axis=-1, keepdims=True)
        mu = jnp.where(first, s_1, s_all - s_1) * inv_n
        d = o - mu
        d2 = d * d
        v_all = jnp.sum(d2, axis=-1, keepdims=True)
        v_1 = jnp.sum(jnp.where(first, d2, 0.0), axis=-1, keepdims=True)
        var = jnp.where(first, v_1, v_all - v_1) * inv_n
        y = d * lax.rsqrt(var + EPS) * gnw_ref[:, sl]
        g = g_ref[0, rows, sl]
        o_ref[0, rows, sl] = (g * jax.nn.sigmoid(g) * y).astype(BF16)


def _retention(rq, rk, rv, rg, gnw):
    B, S, W = rq.shape
    C = RET_CHUNK
    H = RET_HEADS
    log_gamma = np.log1p(-np.power(2.0, -5.0 - np.arange(H, dtype=np.float64)))
    idx = np.arange(C, dtype=np.float64)
    diff = idx[:, None] - idx[None, :]
    dm = np.where(diff >= 0, np.exp(log_gamma[:, None, None] * np.maximum(diff, 0.0)), 0.0)
    lane_head = np.arange(W) // RET_HEAD_DIM
    xi = np.exp(log_gamma[lane_head][None, :] * (idx[:, None] + 1.0))
    zeta = np.exp(log_gamma[lane_head][None, :] * (C - 1.0 - idx[:, None]))
    to_pairs = lambda a: np.ascontiguousarray(a.reshape(C, H // 2, LANES).transpose(1, 0, 2))
    row_head = np.arange(LANES) // RET_HEAD_DIM
    smask = (row_head[:, None] == row_head[None, :]).astype(np.float64)
    sdec = np.stack([np.exp(log_gamma[2 * p + row_head] * C)[:, None] * np.ones((1, LANES))
                     for p in range(H // 2)])
    consts = [jnp.asarray(a, F32) for a in (dm, to_pairs(xi), to_pairs(zeta), sdec, smask)]

    tok = lambda: pl.BlockSpec((1, RET_ROWS, W), lambda b, n: (b, n, 0))
    full = lambda a: pl.BlockSpec(a.shape, lambda b, n: (0,) * a.ndim)
    return pl.pallas_call(
        _retention_kernel,
        grid=(B, S // RET_ROWS),
        in_specs=[tok(), tok(), tok(), tok()] + [full(a) for a in consts] + [full(gnw)],
        out_specs=tok(),
        out_shape=jax.ShapeDtypeStruct((B, S, W), BF16),
        scratch_shapes=[pltpu.VMEM((H // 2, LANES, LANES), F32)],
        compiler_params=pltpu.CompilerParams(
            dimension_semantics=("parallel", "arbitrary"), vmem_limit_bytes=VMEM_LIMIT),
        name="retention",
    )(rq, rk, rv, rg, *consts, gnw)


def _mla_kernel(qi_ref, kj_ref, q_ref, k_ref, vt_ref, bias_ref, o_ref, m_ref, acc_ref, s_ref):
    t = pl.program_id(1)
    i = qi_ref[t]
    j = kj_ref[t]
    tq = q_ref.shape[1]
    n_sub = k_ref.shape[1] // TK_SUB

    @pl.when(j == 0)
    def _():
        m_ref[...] = jnp.full_like(m_ref, -jnp.inf)
        acc_ref[...] = jnp.zeros_like(acc_ref)

    def step(diagonal):
        units = [(sub, hd) for sub in range(n_sub) for hd in range(MLA_HEADS)]

        def scores(unit):
            sub, hd = unit
            sl = slice(hd * HEAD_PAD, (hd + 1) * HEAD_PAD)
            keys = k_ref[0, sub * TK_SUB:(sub + 1) * TK_SUB, sl]
            s = lax.dot_general(keys, q_ref[0, :, sl], (((1,), (1,)), ((), ())),
                                preferred_element_type=F32)
            return s + bias_ref[sub] if diagonal else s

        def stage(n):
            s = scores(units[n])
            s_ref[n % 2] = s
            return jnp.max(s, axis=0, keepdims=True)

        mx_next = stage(0)
        for n, (sub, hd) in enumerate(units):
            mx = mx_next
            if n + 1 < len(units):
                mx_next = stage(n + 1)
            m_prev = m_ref[hd]
            m_new = jnp.maximum(m_prev, mx)
            a = jnp.exp2(m_prev - m_new)
            pr = jnp.exp2(s_ref[n % 2] - m_new[:1]).astype(BF16)
            m_ref[hd] = m_new
            vals_t = vt_ref[0, hd, :, sub * TK_SUB:(sub + 1) * TK_SUB]
            acc_ref[hd] = a[:1] * acc_ref[hd] + jnp.dot(vals_t, pr, preferred_element_type=F32)

    @pl.when(j < i)
    def _():
        step(False)

    @pl.when(j == i)
    def _():
        step(True)

    @pl.when(j == i)
    def _():
        for p in range(MLA_HEADS // 2):
            outs = []
            for hd in (2 * p, 2 * p + 1):
                a = acc_ref[hd]
                outs.append((a[:MLA_V_DIM] / a[MLA_V_DIM:2 * MLA_V_DIM]).T)
            o_ref[0, :, p * LANES:(p + 1) * LANES] = jnp.concatenate(outs, axis=1).astype(BF16)


def _mla(q, k, vt):
    B, S, _ = q.shape
    assert TQ == TK and TK % TK_SUB == 0
    pairs = [(i, j) for i in range(S // TQ) for j in range(i + 1)]
    qi = jnp.asarray(np.array([p[0] for p in pairs], np.int32))
    kj = jnp.asarray(np.array([p[1] for p in pairs], np.int32))
    qpos = np.arange(TQ)[None, None, :]
    kpos = np.arange(TK_SUB)[None, :, None] + TK_SUB * np.arange(TK // TK_SUB)[:, None, None]
    bias = jnp.asarray(np.where(kpos <= qpos, 0.0, -np.inf), F32)
    grid_spec = pltpu.PrefetchScalarGridSpec(
        num_scalar_prefetch=2,
        grid=(B, len(pairs)),
        in_specs=[
            pl.BlockSpec((1, TQ, MLA_QK_WIDTH), lambda b, t, qi, kj: (b, qi[t], 0)),
            pl.BlockSpec((1, TK, MLA_QK_WIDTH), lambda b, t, qi, kj: (b, kj[t], 0)),
            pl.BlockSpec((1, MLA_HEADS, HEAD_PAD, TK), lambda b, t, qi, kj: (b, 0, 0, kj[t])),
            pl.BlockSpec(bias.shape, lambda b, t, qi, kj: (0, 0, 0)),
        ],
        out_specs=pl.BlockSpec((1, TQ, MLA_WIDTH), lambda b, t, qi, kj: (b, qi[t], 0)),
        scratch_shapes=[
            pltpu.VMEM((MLA_HEADS, SUBLANES, TQ), F32),
            pltpu.VMEM((MLA_HEADS, HEAD_PAD, TQ), F32),
            pltpu.VMEM((2, TK_SUB, TQ), F32),
        ],
    )
    return pl.pallas_call(
        _mla_kernel,
        grid_spec=grid_spec,
        out_shape=jax.ShapeDtypeStruct((B, S, MLA_WIDTH), BF16),
        compiler_params=pltpu.CompilerParams(
            dimension_semantics=("parallel", "arbitrary"), vmem_limit_bytes=VMEM_LIMIT),
        name="mla_attention",
    )(qi, kj, q, k, vt, bias)


def _ffn_kernel(x_ref, yr_ref, ym_ref, wo_ref, fnw_ref, wup_ref, cw_ref, cb_ref, wdn_ref, onw_ref,
                o_ref, carry_ref, act_ref):
    tm = x_ref.shape[1]

    @pl.when(pl.program_id(1) == 0)
    def _():
        carry_ref[...] = jnp.zeros_like(carry_ref)

    x1 = (x_ref[0]
          + jnp.dot(yr_ref[0], wo_ref[:RET_WIDTH, :], preferred_element_type=F32)
          + jnp.dot(ym_ref[0], wo_ref[RET_WIDTH:, :], preferred_element_type=F32))
    h = _rms(x1, fnw_ref[...]).astype(BF16)

    first_row = lax.broadcasted_iota(jnp.int32, (SUBLANES, TF), 0) == 0

    def shift_down(v, row0):
        r = pltpu.roll(v, 1, 0)
        return jnp.concatenate([jnp.where(first_row, row0, r[:SUBLANES]), r[SUBLANES:]], axis=0)

    def causal_conv(lo):
        cols = slice(lo, lo + TF)
        u = jnp.dot(h, wup_ref[:, cols], preferred_element_type=F32)
        w0, w1, w2 = cw_ref[0:1, cols], cw_ref[1:2, cols], cw_ref[2:3, cols]
        prev = carry_ref[:, cols]
        p2, p1 = prev[SUBLANES - 2:SUBLANES - 1], prev[SUBLANES - 1:SUBLANES]
        carry_ref[:, cols] = u[tm - SUBLANES:]
        v = shift_down(w0 * u, w0 * p1) + w1 * u
        return shift_down(v, w0 * p2 + w1 * p1) + w2 * u + cb_ref[:, cols]

    for f in range(D_FF // TF):
        gate = causal_conv(f * TF)
        val = causal_conv(D_FF + f * TF)
        act_ref[:, f * TF:(f + 1) * TF] = (gate * jax.nn.sigmoid(gate) * val).astype(BF16)

    x2 = x1 + jnp.dot(act_ref[...], wdn_ref[...], preferred_element_type=F32)
    o_ref[0] = _rms(x2, onw_ref[...])


def _ffn(x, y_ret, y_mla, wo, fnw, wup, cw, cb, wdn, onw):
    B, S, D = x.shape
    tm = TM_FFN
    tok = lambda w: pl.BlockSpec((1, tm, w), lambda b, i: (b, i, 0))
    full = lambda a: pl.BlockSpec(a.shape, lambda b, i: (0,) * a.ndim, pipeline_mode=pl.Buffered(1))
    return pl.pallas_call(
        _ffn_kernel,
        grid=(B, S // tm),
        in_specs=[tok(D), tok(RET_WIDTH), tok(MLA_WIDTH)] + [full(a) for a in (wo, fnw, wup, cw, cb, wdn, onw)],
        out_specs=tok(D),
        out_shape=jax.ShapeDtypeStruct((B, S, D), F32),
        scratch_shapes=[
            pltpu.VMEM((SUBLANES, 2 * D_FF), F32),
            pltpu.VMEM((tm, D_FF), BF16),
        ],
        compiler_params=pltpu.CompilerParams(
            dimension_semantics=("parallel", "arbitrary"), vmem_limit_bytes=VMEM_LIMIT),
        name="outproj_ffn",
    )(x, y_ret, y_mla, wo, fnw, wup, cw, cb, wdn, onw)


def _rotary_lanes():
    lane = np.arange(LANES)

    def rows(d, active, first_half, freq_idx):
        inv_freq = ROPE_BASE ** (-jnp.arange(0, d, 2, dtype=F32) / d)
        freq = jnp.where(active, inv_freq[freq_idx], 0.0)
        return freq, jnp.asarray(np.where(first_half, -1.0, 1.0), F32)

    half_r = RET_HEAD_DIM // 2
    fr, gr = rows(RET_HEAD_DIM, np.ones(LANES, bool), (lane % RET_HEAD_DIM) < half_r, lane % half_r)
    half_m = MLA_ROPE_DIM // 2
    in_rope = (lane >= MLA_NOPE_DIM) & (lane < MLA_NOPE_DIM + MLA_ROPE_DIM)
    fm, gm = rows(MLA_ROPE_DIM, in_rope, lane < MLA_NOPE_DIM + half_m, (lane - MLA_NOPE_DIM) % half_m)
    return jnp.stack([fr, gr, fm, gm]).astype(F32)


def _layer(x, pos, rot, attn_norm_w, w_in, ret_gn_w, mla_q_norm_w, w_uq, mla_kv_norm_w, w_ukv,
           w_out, ffn_norm_w, w_up, conv_w, conv_b, w_down, out_norm_w):
    D = x.shape[-1]
    qk = MLA_NOPE_DIM + MLA_ROPE_DIM
    kpe_cols = jnp.zeros((D, LANES), F32).at[:, MLA_NOPE_DIM:qk].set(w_in[:, -MLA_ROPE_DIM:])
    win = jnp.concatenate([w_in[:, :-MLA_ROPE_DIM], kpe_cols], axis=1).astype(BF16)
    wq = jnp.pad(w_uq.reshape(MLA_Q_RANK, MLA_HEADS, qk), ((0, 0), (0, 0), (0, HEAD_PAD - qk)))
    wq = wq.reshape(MLA_Q_RANK, MLA_QK_WIDTH).astype(BF16)
    wkv = w_ukv.reshape(MLA_KV_RANK, MLA_HEADS, MLA_NOPE_DIM + MLA_V_DIM)
    wk = jnp.pad(wkv[:, :, :MLA_NOPE_DIM], ((0, 0), (0, 0), (0, HEAD_PAD - MLA_NOPE_DIM)))
    wk = wk.reshape(MLA_KV_RANK, MLA_QK_WIDTH).astype(BF16)
    wv = wkv[:, :, MLA_NOPE_DIM:].reshape(MLA_KV_RANK, MLA_WIDTH).T.astype(BF16)
    row = lambda a: a.reshape(1, -1).astype(F32)

    rq, rk, rv, rg, q, k, v = _inproj(x, pos, rot, row(attn_norm_w), win, row(mla_q_norm_w), wq,
                                      row(mla_kv_norm_w), wk, wv)
    y_ret = _retention(rq, rk, rv, rg, row(ret_gn_w))
    y_mla = _mla(q, k, v)
    return _ffn(x, y_ret, y_mla, w_out.astype(BF16), row(ffn_norm_w), w_up.astype(BF16),
                conv_w.astype(F32), row(conv_b), w_down.astype(BF16), row(out_norm_w))


def kernel(x, positions, attn_norm_w, w_in, ret_gn_w, mla_q_norm_w, w_uq, mla_kv_norm_w, w_ukv,
           w_out, ffn_norm_w, w_up, conv_w, conv_b, w_down, final_norm_w):
    depth = w_in.shape[0]
    assert depth == 1, "the final RMSNorm is fused into the (single) layer's FFN kernel"
    pos = jnp.broadcast_to(positions.astype(F32)[..., None], positions.shape + (LANES,))
    return _layer(x, pos, _rotary_lanes(), attn_norm_w[0], w_in[0], ret_gn_w[0], mla_q_norm_w[0], w_uq[0],
                  mla_kv_norm_w[0], w_ukv[0], w_out[0], ffn_norm_w[0], w_up[0], conv_w[0],
                  conv_b[0], w_down[0], final_norm_w)
```

```python
import jax
import jax.numpy as jnp
import numpy as np
from jax import lax
from jax.experimental import pallas as pl
from jax.experimental.pallas import tpu as pltpu

F32 = jnp.float32
BF16 = jnp.bfloat16

D_MODEL = 1024
RET_HEADS = 8
RET_HEAD_DIM = 64
RET_WIDTH = RET_HEADS * RET_HEAD_DIM
MLA_HEADS = 8
MLA_NOPE_DIM = 64
MLA_ROPE_DIM = 32
MLA_V_DIM = 64
MLA_Q_RANK = 256
MLA_KV_RANK = 128
MLA_WIDTH = MLA_HEADS * MLA_V_DIM
D_FF = 2816
CONV_WIDTH = 3
ROPE_BASE = 10000.0
EPS = 1e-6
LOG2_E = 1.4426950408889634

LANES = 128
SUBLANES = 8
HEAD_PAD = LANES
MLA_QK_WIDTH = MLA_HEADS * HEAD_PAD
IN_PAD_WIDTH = 4 * RET_WIDTH + MLA_Q_RANK + MLA_KV_RANK + LANES

TM_IN = 512
RET_CHUNK = 256
RET_ROWS = 1024
TQ = 1024
TK = 2048
TK_SUB = 512
TM_FFN = 512
TF = 256
VMEM_LIMIT = 56 * 1024 * 1024


def _rms(x, w):
    return x * lax.rsqrt(jnp.mean(x * x, axis=-1, keepdims=True) + EPS) * w


def _inproj_kernel(x_ref, pos_ref, rot_ref, anw_ref, win_ref, qnw_ref, wq_ref,
                   kvnw_ref, wk_ref, wvt_ref,
                   rq_ref, rk_ref, rv_ref, rg_ref, q_ref, k_ref, vt_ref):
    tm = x_ref.shape[1]
    h = _rms(x_ref[0], anw_ref[...]).astype(BF16)

    lane = lax.broadcasted_iota(jnp.int32, (tm, LANES), 1)
    ret_first_half = (lane % RET_HEAD_DIM) < (RET_HEAD_DIM // 2)
    mla_first_half = lane < (MLA_NOPE_DIM + MLA_ROPE_DIM // 2)
    def seg(lo, width):
        return jnp.dot(h, win_ref[:, lo:lo + width], preferred_element_type=F32)

    cq = seg(4 * RET_WIDTH, MLA_Q_RANK)
    tail = seg(4 * RET_WIDTH + MLA_Q_RANK, MLA_KV_RANK + LANES)
    pq = seg(0, RET_WIDTH)
    cqn = _rms(cq, qnw_ref[...]).astype(BF16)
    ckv_norm = _rms(tail[:, :MLA_KV_RANK], kvnw_ref[...])
    ckvn = ckv_norm.astype(BF16)
    ckvn_t = ckv_norm.T.astype(BF16)
    pk = seg(RET_WIDTH, RET_WIDTH)
    qf = jnp.dot(cqn, wq_ref[...], preferred_element_type=F32)
    kf = jnp.dot(ckvn, wk_ref[...], preferred_element_type=F32)
    vt = jnp.dot(wvt_ref[...], ckvn_t, preferred_element_type=F32)
    rv_ref[0] = seg(2 * RET_WIDTH, RET_WIDTH).astype(BF16)
    rg_ref[0] = seg(3 * RET_WIDTH, RET_WIDTH)

    pos = pos_ref[0]
    ang_r = pos * rot_ref[0:1]
    cr, sr = jnp.cos(ang_r), jnp.sin(ang_r) * rot_ref[1:2]
    ang_m = pos * rot_ref[2:3]
    cm, sm = jnp.cos(ang_m), jnp.sin(ang_m) * rot_ref[3:4]

    def rope_ret(p):
        rot = jnp.where(ret_first_half, pltpu.roll(p, LANES - 32, 1), pltpu.roll(p, 32, 1))
        return p * cr + rot * sr

    def rope_mla(p):
        rot = jnp.where(mla_first_half, pltpu.roll(p, LANES - 16, 1), pltpu.roll(p, 16, 1))
        return p * cm + rot * sm

    kpe = rope_mla(tail[:, MLA_KV_RANK:])
    for g in range(RET_WIDTH // LANES):
        sl = slice(g * LANES, (g + 1) * LANES)
        rq_ref[0, :, sl] = rope_ret(pq[:, sl]).astype(BF16)
        rk_ref[0, :, sl] = (rope_ret(pk[:, sl]) * (RET_HEAD_DIM ** -0.5)).astype(BF16)
    scale = (MLA_NOPE_DIM + MLA_ROPE_DIM) ** -0.5 * LOG2_E
    for hd in range(MLA_HEADS):
        sl = slice(hd * HEAD_PAD, (hd + 1) * HEAD_PAD)
        q_ref[0, :, sl] = (rope_mla(qf[:, sl]) * scale).astype(BF16)
        k_ref[0, :, sl] = (kf[:, sl] + kpe).astype(BF16)
        vt_ref[0, hd, :MLA_V_DIM, :] = vt[hd * MLA_V_DIM:(hd + 1) * MLA_V_DIM].astype(BF16)
        vt_ref[0, hd, MLA_V_DIM:, :] = jnp.ones((HEAD_PAD - MLA_V_DIM, tm), BF16)


def _inproj(x, pos, rot, anw, win, qnw, wq, kvnw, wk, wv):
    B, S, D = x.shape
    tm = TM_IN
    grid = (B, S // tm)
    tok = lambda w: pl.BlockSpec((1, tm, w), lambda b, i: (b, i, 0))
    full = lambda a: pl.BlockSpec(a.shape, lambda b, i: (0,) * a.ndim)
    outs = [
        jax.ShapeDtypeStruct((B, S, RET_WIDTH), BF16),
        jax.ShapeDtypeStruct((B, S, RET_WIDTH), BF16),
        jax.ShapeDtypeStruct((B, S, RET_WIDTH), BF16),
        jax.ShapeDtypeStruct((B, S, RET_WIDTH), F32),
        jax.ShapeDtypeStruct((B, S, MLA_QK_WIDTH), BF16),
        jax.ShapeDtypeStruct((B, S, MLA_QK_WIDTH), BF16),
        jax.ShapeDtypeStruct((B, MLA_HEADS, HEAD_PAD, S), BF16),
    ]
    vt_spec = pl.BlockSpec((1, MLA_HEADS, HEAD_PAD, tm), lambda b, i: (b, 0, 0, i))
    return pl.pallas_call(
        _inproj_kernel,
        grid=grid,
        in_specs=[tok(D), tok(LANES)] + [full(a) for a in (rot, anw, win, qnw, wq, kvnw, wk, wv)],
        out_specs=[tok(o.shape[-1]) for o in outs[:-1]] + [vt_spec],
        out_shape=outs,
        compiler_params=pltpu.CompilerParams(
            dimension_semantics=("parallel", "parallel"), vmem_limit_bytes=VMEM_LIMIT),
        name="inproj",
    )(x, pos, rot, anw, win, qnw, wq, kvnw, wk, wv)


def _retention_kernel(q_ref, k_ref, v_ref, g_ref, dm_ref, xi_ref, zeta_ref, sdec_ref, smask_ref,
                      gnw_ref, o_ref, r_ref):
    C = RET_CHUNK

    @pl.when(pl.program_id(1) == 0)
    def _():
        r_ref[...] = jnp.zeros_like(r_ref)

    lane = lax.broadcasted_iota(jnp.int32, (C, LANES), 1)
    first = lane < RET_HEAD_DIM
    inv_n = 1.0 / RET_HEAD_DIM
    for c, p in [(c, p) for c in range(q_ref.shape[1] // C) for p in range(RET_HEADS // 2)]:
        rows = slice(c * C, (c + 1) * C)
        sl = slice(p * LANES, (p + 1) * LANES)
        q = q_ref[0, rows, sl]
        k = k_ref[0, rows, sl]
        v = v_ref[0, rows, sl]
        zero = jnp.zeros_like(q)
        inner = []
        for hh, qm in enumerate((jnp.where(first, q, zero), jnp.where(first, zero, q))):
            s = lax.dot_general(qm, k, (((1,), (1,)), ((), ())), preferred_element_type=F32)
            s = (s * dm_ref[2 * p + hh]).astype(BF16)
            inner.append(jnp.dot(s, v, preferred_element_type=F32))
        o = jnp.where(first, inner[0], inner[1])
        r_prev = r_ref[p]
        o = o + jnp.dot(q, r_prev.astype(BF16), preferred_element_type=F32) * xi_ref[p]
        kz = (k.astype(F32) * zeta_ref[p]).T.astype(BF16)
        upd = jnp.dot(kz, v, preferred_element_type=F32)
        r_ref[p] = sdec_ref[p] * r_prev + smask_ref[...] * upd

        s_all = jnp.sum(o, axis=-1, keepdims=True)
        s_1 = jnp.sum(jnp.where(first, o, 0.0), axis=-1, keepdims=True)
        mu = jnp.where(first, s_1, s_all - s_1) * inv_n
        d = o - mu
        d2 = d * d
        v_all = jnp.sum(d2, axis=-1, keepdims=True)
        v_1 = jnp.sum(jnp.where(first, d2, 0.0), axis=-1, keepdims=True)
        var = jnp.where(first, v_1, v_all - v_1) * inv_n
        y = d * lax.rsqrt(var + EPS) * gnw_ref[:, sl]
        g = g_ref[0, rows, sl]
        o_ref[0, rows, sl] = (g * jax.nn.sigmoid(g) * y).astype(BF16)


def _retention(rq, rk, rv, rg, gnw):
    B, S, W = rq.shape
    C = RET_CHUNK
    H = RET_HEADS
    log_gamma = np.log1p(-np.power(2.0, -5.0 - np.arange(H, dtype=np.float64)))
    idx = np.arange(C, dtype=np.float64)
    diff = idx[:, None] - idx[None, :]
    dm = np.where(diff >= 0, np.exp(log_gamma[:, None, None] * np.maximum(diff, 0.0)), 0.0)
    lane_head = np.arange(W) // RET_HEAD_DIM
    xi = np.exp(log_gamma[lane_head][None, :] * (idx[:, None] + 1.0))
    zeta = np.exp(log_gamma[lane_head][None, :] * (C - 1.0 - idx[:, None]))
    to_pairs = lambda a: np.ascontiguousarray(a.reshape(C, H // 2, LANES).transpose(1, 0, 2))
    row_head = np.arange(LANES) // RET_HEAD_DIM
    smask = (row_head[:, None] == row_head[None, :]).astype(np.float64)
    sdec = np.stack([np.exp(log_gamma[2 * p + row_head] * C)[:, None] * np.ones((1, LANES))
                     for p in range(H // 2)])
    consts = [jnp.asarray(a, F32) for a in (dm, to_pairs(xi), to_pairs(zeta), sdec, smask)]

    tok = lambda: pl.BlockSpec((1, RET_ROWS, W), lambda b, n: (b, n, 0))
    full = lambda a: pl.BlockSpec(a.shape, lambda b, n: (0,) * a.ndim)
    return pl.pallas_call(
        _retention_kernel,
        grid=(B, S // RET_ROWS),
        in_specs=[tok(), tok(), tok(), tok()] + [full(a) for a in consts] + [full(gnw)],
        out_specs=tok(),
        out_shape=jax.ShapeDtypeStruct((B, S, W), BF16),
        scratch_shapes=[pltpu.VMEM((H // 2, LANES, LANES), F32)],
        compiler_params=pltpu.CompilerParams(
            dimension_semantics=("parallel", "arbitrary"), vmem_limit_bytes=VMEM_LIMIT),
        name="retention",
    )(rq, rk, rv, rg, *consts, gnw)


def _mla_kernel(qi_ref, kj_ref, kd_ref, q_ref, k_ref, vt_ref, bias_ref, o_ref, m_ref, acc_ref, s_ref):
    t = pl.program_id(1)
    j = kj_ref[t]
    diag_block = kd_ref[t]
    tq = q_ref.shape[1]
    blocks = k_ref.shape[1] // tq
    subs = tq // TK_SUB

    @pl.when(j == 0)
    def _():
        m_ref[...] = jnp.full_like(m_ref, -jnp.inf)
        acc_ref[...] = jnp.zeros_like(acc_ref)

    def step(d):
        units = [(r * subs + c, hd, c if r == d else None)
                 for r in range(min(d + 1, blocks)) for c in range(subs) for hd in range(MLA_HEADS)]

        def scores(unit):
            sub, hd, bias_idx = unit
            sl = slice(hd * HEAD_PAD, (hd + 1) * HEAD_PAD)
            keys = k_ref[0, sub * TK_SUB:(sub + 1) * TK_SUB, sl]
            s = lax.dot_general(keys, q_ref[0, :, sl], (((1,), (1,)), ((), ())),
                                preferred_element_type=F32)
            return s if bias_idx is None else s + bias_ref[bias_idx]

        def stage(n):
            s = scores(units[n])
            s_ref[n % 2] = s
            return jnp.max(s, axis=0, keepdims=True)

        mx_next = stage(0)
        for n, (sub, hd, _) in enumerate(units):
            mx = mx_next
            if n + 1 < len(units):
                mx_next = stage(n + 1)
            m_prev = m_ref[hd]
            m_new = jnp.maximum(m_prev, mx)
            a = jnp.exp2(m_prev - m_new)
            pr = jnp.exp2(s_ref[n % 2] - m_new[:1]).astype(BF16)
            m_ref[hd] = m_new
            vals_t = vt_ref[0, hd, :, sub * TK_SUB:(sub + 1) * TK_SUB]
            acc_ref[hd] = a[:1] * acc_ref[hd] + jnp.dot(vals_t, pr, preferred_element_type=F32)

    for d in range(blocks + 1):
        pl.when(diag_block == d)(lambda d=d: step(d))

    @pl.when(diag_block < blocks)
    def _():
        for p in range(MLA_HEADS // 2):
            outs = []
            for hd in (2 * p, 2 * p + 1):
                a = acc_ref[hd]
                outs.append((a[:MLA_V_DIM] / a[MLA_V_DIM:2 * MLA_V_DIM]).T)
            o_ref[0, :, p * LANES:(p + 1) * LANES] = jnp.concatenate(outs, axis=1).astype(BF16)


def _mla(q, k, vt):
    B, S, _ = q.shape
    assert TK % TQ == 0 and TQ % TK_SUB == 0
    blocks = TK // TQ
    pairs = [(i, j) for i in range(S // TQ) for j in range(i // blocks + 1)]
    qi = jnp.asarray(np.array([i for i, _ in pairs], np.int32))
    kj = jnp.asarray(np.array([j for _, j in pairs], np.int32))
    kd = jnp.asarray(np.array([min(i - blocks * j, blocks) for i, j in pairs], np.int32))
    qpos = np.arange(TQ)[None, None, :]
    kpos = np.arange(TK_SUB)[None, :, None] + TK_SUB * np.arange(TQ // TK_SUB)[:, None, None]
    bias = jnp.asarray(np.where(kpos <= qpos, 0.0, -np.inf), F32)
    grid_spec = pltpu.PrefetchScalarGridSpec(
        num_scalar_prefetch=3,
        grid=(B, len(pairs)),
        in_specs=[
            pl.BlockSpec((1, TQ, MLA_QK_WIDTH), lambda b, t, qi, kj, kd: (b, qi[t], 0)),
            pl.BlockSpec((1, TK, MLA_QK_WIDTH), lambda b, t, qi, kj, kd: (b, kj[t], 0)),
            pl.BlockSpec((1, MLA_HEADS, HEAD_PAD, TK), lambda b, t, qi, kj, kd: (b, 0, 0, kj[t])),
            pl.BlockSpec(bias.shape, lambda b, t, qi, kj, kd: (0, 0, 0)),
        ],
        out_specs=pl.BlockSpec((1, TQ, MLA_WIDTH), lambda b, t, qi, kj, kd: (b, qi[t], 0)),
        scratch_shapes=[
            pltpu.VMEM((MLA_HEADS, SUBLANES, TQ), F32),
            pltpu.VMEM((MLA_HEADS, HEAD_PAD, TQ), F32),
            pltpu.VMEM((2, TK_SUB, TQ), F32),
        ],
    )
    return pl.pallas_call(
        _mla_kernel,
        grid_spec=grid_spec,
        out_shape=jax.ShapeDtypeStruct((B, S, MLA_WIDTH), BF16),
        compiler_params=pltpu.CompilerParams(
            dimension_semantics=("parallel", "arbitrary"), vmem_limit_bytes=VMEM_LIMIT),
        name="mla_attention",
    )(qi, kj, kd, q, k, vt, bias)


def _ffn_kernel(x_ref, yr_ref, ym_ref, wo_ref, fnw_ref, wup_ref, cw_ref, cb_ref, wdn_ref, onw_ref,
                o_ref, carry_ref, act_ref):
    tm = x_ref.shape[1]

    @pl.when(pl.program_id(1) == 0)
    def _():
        carry_ref[...] = jnp.zeros_like(carry_ref)

    x1 = (x_ref[0]
          + jnp.dot(yr_ref[0], wo_ref[:RET_WIDTH, :], preferred_element_type=F32)
          + jnp.dot(ym_ref[0], wo_ref[RET_WIDTH:, :], preferred_element_type=F32))
    h = _rms(x1, fnw_ref[...]).astype(BF16)

    first_row = lax.broadcasted_iota(jnp.int32, (SUBLANES, TF), 0) == 0

    def shift_down(v, row0):
        r = pltpu.roll(v, 1, 0)
        return jnp.concatenate([jnp.where(first_row, row0, r[:SUBLANES]), r[SUBLANES:]], axis=0)

    def causal_conv(lo):
        cols = slice(lo, lo + TF)
        u = jnp.dot(h, wup_ref[:, cols], preferred_element_type=F32)
        w0, w1, w2 = cw_ref[0:1, cols], cw_ref[1:2, cols], cw_ref[2:3, cols]
        prev = carry_ref[:, cols]
        p2, p1 = prev[SUBLANES - 2:SUBLANES - 1], prev[SUBLANES - 1:SUBLANES]
        carry_ref[:, cols] = u[tm - SUBLANES:]
        v = shift_down(w0 * u, w0 * p1) + w1 * u
        return shift_down(v, w0 * p2 + w1 * p1) + w2 * u + cb_ref[:, cols]

    for f in range(D_FF // TF):
        gate = causal_conv(f * TF)
        val = causal_conv(D_FF + f * TF)
        act_ref[:, f * TF:(f + 1) * TF] = (gate * jax.nn.sigmoid(gate) * val).astype(BF16)

    x2 = x1 + jnp.dot(act_ref[...], wdn_ref[...], preferred_element_type=F32)
    o_ref[0] = _rms(x2, onw_ref[...])


def _ffn(x, y_ret, y_mla, wo, fnw, wup, cw, cb, wdn, onw):
    B, S, D = x.shape
    tm = TM_FFN
    tok = lambda w: pl.BlockSpec((1, tm, w), lambda b, i: (b, i, 0))
    full = lambda a: pl.BlockSpec(a.shape, lambda b, i: (0,) * a.ndim, pipeline_mode=pl.Buffered(1))
    return pl.pallas_call(
        _ffn_kernel,
        grid=(B, S // tm),
        in_specs=[tok(D), tok(RET_WIDTH), tok(MLA_WIDTH)] + [full(a) for a in (wo, fnw, wup, cw, cb, wdn, onw)],
        out_specs=tok(D),
        out_shape=jax.ShapeDtypeStruct((B, S, D), F32),
        scratch_shapes=[
            pltpu.VMEM((SUBLANES, 2 * D_FF), F32),
            pltpu.VMEM((tm, D_FF), BF16),
        ],
        compiler_params=pltpu.CompilerParams(
            dimension_semantics=("parallel", "arbitrary"), vmem_limit_bytes=VMEM_LIMIT),
        name="outproj_ffn",
    )(x, y_ret, y_mla, wo, fnw, wup, cw, cb, wdn, onw)


def _rotary_lanes():
    lane = np.arange(LANES)

    def rows(d, active, first_half, freq_idx):
        inv_freq = ROPE_BASE ** (-jnp.arange(0, d, 2, dtype=F32) / d)
        freq = jnp.where(active, inv_freq[freq_idx], 0.0)
        return freq, jnp.asarray(np.where(first_half, -1.0, 1.0), F32)

    half_r = RET_HEAD_DIM // 2
    fr, gr = rows(RET_HEAD_DIM, np.ones(LANES, bool), (lane % RET_HEAD_DIM) < half_r, lane % half_r)
    half_m = MLA_ROPE_DIM // 2
    in_rope = (lane >= MLA_NOPE_DIM) & (lane < MLA_NOPE_DIM + MLA_ROPE_DIM)
    fm, gm = rows(MLA_ROPE_DIM, in_rope, lane < MLA_NOPE_DIM + half_m, (lane - MLA_NOPE_DIM) % half_m)
    return jnp.stack([fr, gr, fm, gm]).astype(F32)


def _layer(x, pos, rot, attn_norm_w, w_in, ret_gn_w, mla_q_norm_w, w_uq, mla_kv_norm_w, w_ukv,
           w_out, ffn_norm_w, w_up, conv_w, conv_b, w_down, out_norm_w):
    D = x.shape[-1]
    qk = MLA_NOPE_DIM + MLA_ROPE_DIM
    kpe_cols = jnp.zeros((D, LANES), F32).at[:, MLA_NOPE_DIM:qk].set(w_in[:, -MLA_ROPE_DIM:])
    win = jnp.concatenate([w_in[:, :-MLA_ROPE_DIM], kpe_cols], axis=1).astype(BF16)
    wq = jnp.pad(w_uq.reshape(MLA_Q_RANK, MLA_HEADS, qk), ((0, 0), (0, 0), (0, HEAD_PAD - qk)))
    wq = wq.reshape(MLA_Q_RANK, MLA_QK_WIDTH).astype(BF16)
    wkv = w_ukv.reshape(MLA_KV_RANK, MLA_HEADS, MLA_NOPE_DIM + MLA_V_DIM)
    wk = jnp.pad(wkv[:, :, :MLA_NOPE_DIM], ((0, 0), (0, 0), (0, HEAD_PAD - MLA_NOPE_DIM)))
    wk = wk.reshape(MLA_KV_RANK, MLA_QK_WIDTH).astype(BF16)
    wv = wkv[:, :, MLA_NOPE_DIM:].reshape(MLA_KV_RANK, MLA_WIDTH).T.astype(BF16)
    row = lambda a: a.reshape(1, -1).astype(F32)

    rq, rk, rv, rg, q, k, v = _inproj(x, pos, rot, row(attn_norm_w), win, row(mla_q_norm_w), wq,
                                      row(mla_kv_norm_w), wk, wv)
    y_ret = _retention(rq, rk, rv, rg, row(ret_gn_w))
    y_mla = _mla(q, k, v)
    return _ffn(x, y_ret, y_mla, w_out.astype(BF16), row(ffn_norm_w), w_up.astype(BF16),
                conv_w.astype(F32), row(conv_b), w_down.astype(BF16), row(out_norm_w))


def kernel(x, positions, attn_norm_w, w_in, ret_gn_w, mla_q_norm_w, w_uq, mla_kv_norm_w, w_ukv,
           w_out, ffn_norm_w, w_up, conv_w, conv_b, w_down, final_norm_w):
    depth = w_in.shape[0]
    assert depth == 1, "the final RMSNorm is fused into the (single) layer's FFN kernel"
    pos = jnp.broadcast_to(positions.astype(F32)[..., None], positions.shape + (LANES,))
    return _layer(x, pos, _rotary_lanes(), attn_norm_w[0], w_in[0], ret_gn_w[0], mla_q_norm_w[0], w_uq[0],
                  mla_kv_norm_w[0], w_ukv[0], w_out[0], ffn_norm_w[0], w_up[0], conv_w[0],
                  conv_b[0], w_down[0], final_norm_w)
```

```python
import jax
import jax.numpy as jnp
import numpy as np
from jax import lax
from jax.experimental import pallas as pl
from jax.experimental.pallas import tpu as pltpu

F32 = jnp.float32
BF16 = jnp.bfloat16

D_MODEL = 1024
RET_HEADS = 8
RET_HEAD_DIM = 64
RET_WIDTH = RET_HEADS * RET_HEAD_DIM
MLA_HEADS = 8
MLA_NOPE_DIM = 64
MLA_ROPE_DIM = 32
MLA_V_DIM = 64
MLA_Q_RANK = 256
MLA_KV_RANK = 128
MLA_WIDTH = MLA_HEADS * MLA_V_DIM
D_FF = 2816
CONV_WIDTH = 3
ROPE_BASE = 10000.0
EPS = 1e-6
LOG2_E = 1.4426950408889634

LANES = 128
SUBLANES = 8
HEAD_PAD = LANES
MLA_QK_WIDTH = MLA_HEADS * HEAD_PAD
IN_PAD_WIDTH = 4 * RET_WIDTH + MLA_Q_RANK + MLA_KV_RANK + LANES

TM_IN = 512
RET_CHUNK = 256
RET_ROWS = 1024
TQ = 1024
TK = 1024
TK_SUB = 512
TM_FFN = 512
TF = 256
VMEM_LIMIT = 56 * 1024 * 1024


def _rms(x, w):
    return x * lax.rsqrt(jnp.mean(x * x, axis=-1, keepdims=True) + EPS) * w


def _inproj_kernel(x_ref, pos_ref, rot_ref, anw_ref, win_ref, qnw_ref, wq_ref,
                   kvnw_ref, wk_ref, wvt_ref,
                   rq_ref, rk_ref, rv_ref, rg_ref, q_ref, k_ref, vt_ref):
    tm = x_ref.shape[1]
    h = _rms(x_ref[0], anw_ref[...]).astype(BF16)

    lane = lax.broadcasted_iota(jnp.int32, (tm, LANES), 1)
    ret_first_half = (lane % RET_HEAD_DIM) < (RET_HEAD_DIM // 2)
    mla_first_half = lane < (MLA_NOPE_DIM + MLA_ROPE_DIM // 2)
    pos = pos_ref[0]
    ang_r = pos * rot_ref[0:1]
    cr, sr = jnp.cos(ang_r), jnp.sin(ang_r) * rot_ref[1:2]
    ang_m = pos * rot_ref[2:3]
    cm, sm = jnp.cos(ang_m), jnp.sin(ang_m) * rot_ref[3:4]

    def rope_ret(p):
        rot = jnp.where(ret_first_half, pltpu.roll(p, LANES - 32, 1), pltpu.roll(p, 32, 1))
        return p * cr + rot * sr

    def rope_mla(p):
        rot = jnp.where(mla_first_half, pltpu.roll(p, LANES - 16, 1), pltpu.roll(p, 16, 1))
        return p * cm + rot * sm

    def seg(lo, width):
        return jnp.dot(h, win_ref[:, lo:lo + width], preferred_element_type=F32)

    cq = seg(4 * RET_WIDTH, MLA_Q_RANK)
    tail = seg(4 * RET_WIDTH + MLA_Q_RANK, MLA_KV_RANK + LANES)
    pq = seg(0, RET_WIDTH)
    cqn = _rms(cq, qnw_ref[...]).astype(BF16)
    ckv_norm = _rms(tail[:, :MLA_KV_RANK], kvnw_ref[...])
    ckvn = ckv_norm.astype(BF16)
    ckvn_t = ckv_norm.T.astype(BF16)
    kpe = rope_mla(tail[:, MLA_KV_RANK:])
    pk = seg(RET_WIDTH, RET_WIDTH)
    qf = jnp.dot(cqn, wq_ref[...], preferred_element_type=F32)
    kf = jnp.dot(ckvn, wk_ref[...], preferred_element_type=F32)
    vt = jnp.dot(wvt_ref[...], ckvn_t, preferred_element_type=F32)
    rv_ref[0] = seg(2 * RET_WIDTH, RET_WIDTH).astype(BF16)
    rg_ref[0] = seg(3 * RET_WIDTH, RET_WIDTH)
    for g in range(RET_WIDTH // LANES):
        sl = slice(g * LANES, (g + 1) * LANES)
        rq_ref[0, :, sl] = rope_ret(pq[:, sl]).astype(BF16)
        rk_ref[0, :, sl] = (rope_ret(pk[:, sl]) * (RET_HEAD_DIM ** -0.5)).astype(BF16)
    scale = (MLA_NOPE_DIM + MLA_ROPE_DIM) ** -0.5 * LOG2_E
    for hd in range(MLA_HEADS):
        sl = slice(hd * HEAD_PAD, (hd + 1) * HEAD_PAD)
        q_ref[0, :, sl] = (rope_mla(qf[:, sl]) * scale).astype(BF16)
        k_ref[0, :, sl] = (kf[:, sl] + kpe).astype(BF16)
        vt_ref[0, hd, :MLA_V_DIM, :] = vt[hd * MLA_V_DIM:(hd + 1) * MLA_V_DIM].astype(BF16)
        vt_ref[0, hd, MLA_V_DIM:, :] = jnp.ones((HEAD_PAD - MLA_V_DIM, tm), BF16)


def _inproj(x, pos, rot, anw, win, qnw, wq, kvnw, wk, wv):
    B, S, D = x.shape
    tm = TM_IN
    grid = (B, S // tm)
    tok = lambda w: pl.BlockSpec((1, tm, w), lambda b, i: (b, i, 0))
    full = lambda a: pl.BlockSpec(a.shape, lambda b, i: (0,) * a.ndim)
    outs = [
        jax.ShapeDtypeStruct((B, S, RET_WIDTH), BF16),
        jax.ShapeDtypeStruct((B, S, RET_WIDTH), BF16),
        jax.ShapeDtypeStruct((B, S, RET_WIDTH), BF16),
        jax.ShapeDtypeStruct((B, S, RET_WIDTH), F32),
        jax.ShapeDtypeStruct((B, S, MLA_QK_WIDTH), BF16),
        jax.ShapeDtypeStruct((B, S, MLA_QK_WIDTH), BF16),
        jax.ShapeDtypeStruct((B, MLA_HEADS, HEAD_PAD, S), BF16),
    ]
    vt_spec = pl.BlockSpec((1, MLA_HEADS, HEAD_PAD, tm), lambda b, i: (b, 0, 0, i))
    return pl.pallas_call(
        _inproj_kernel,
        grid=grid,
        in_specs=[tok(D), tok(LANES)] + [full(a) for a in (rot, anw, win, qnw, wq, kvnw, wk, wv)],
        out_specs=[tok(o.shape[-1]) for o in outs[:-1]] + [vt_spec],
        out_shape=outs,
        compiler_params=pltpu.CompilerParams(
            dimension_semantics=("parallel", "parallel"), vmem_limit_bytes=VMEM_LIMIT),
        name="inproj",
    )(x, pos, rot, anw, win, qnw, wq, kvnw, wk, wv)


def _retention_kernel(q_ref, k_ref, v_ref, g_ref, dm_ref, xi_ref, zeta_ref, sdec_ref, smask_ref,
                      gnw_ref, o_ref, r_ref):
    C = RET_CHUNK

    @pl.when(pl.program_id(1) == 0)
    def _():
        r_ref[...] = jnp.zeros_like(r_ref)

    lane = lax.broadcasted_iota(jnp.int32, (C, LANES), 1)
    first = lane < RET_HEAD_DIM
    inv_n = 1.0 / RET_HEAD_DIM
    for c, p in [(c, p) for c in range(q_ref.shape[1] // C) for p in range(RET_HEADS // 2)]:
        rows = slice(c * C, (c + 1) * C)
        sl = slice(p * LANES, (p + 1) * LANES)
        q = q_ref[0, rows, sl]
        k = k_ref[0, rows, sl]
        v = v_ref[0, rows, sl]
        zero = jnp.zeros_like(q)
        inner = []
        for hh, qm in enumerate((jnp.where(first, q, zero), jnp.where(first, zero, q))):
            s = lax.dot_general(qm, k, (((1,), (1,)), ((), ())), preferred_element_type=F32)
            s = (s * dm_ref[2 * p + hh]).astype(BF16)
            inner.append(jnp.dot(s, v, preferred_element_type=F32))
        o = jnp.where(first, inner[0], inner[1])
        r_prev = r_ref[p]
        o = o + jnp.dot(q, r_prev.astype(BF16), preferred_element_type=F32) * xi_ref[p]
        kz = (k.astype(F32) * zeta_ref[p]).T.astype(BF16)
        upd = jnp.dot(kz, v, preferred_element_type=F32)
        r_ref[p] = sdec_ref[p] * r_prev + smask_ref[...] * upd

        s_all = jnp.sum(o, axis=-1, keepdims=True)
        s_1 = jnp.sum(jnp.where(first, o, 0.0), axis=-1, keepdims=True)
        mu = jnp.where(first, s_1, s_all - s_1) * inv_n
        d = o - mu
        d2 = d * d
        v_all = jnp.sum(d2, axis=-1, keepdims=True)
        v_1 = jnp.sum(jnp.where(first, d2, 0.0), axis=-1, keepdims=True)
        var = jnp.where(first, v_1, v_all - v_1) * inv_n
        y = d * lax.rsqrt(var + EPS) * gnw_ref[:, sl]
        g = g_ref[0, rows, sl]
        o_ref[0, rows, sl] = (g * jax.nn.sigmoid(g) * y).astype(BF16)


def _retention(rq, rk, rv, rg, gnw):
    B, S, W = rq.shape
    C = RET_CHUNK
    H = RET_HEADS
    log_gamma = np.log1p(-np.power(2.0, -5.0 - np.arange(H, dtype=np.float64)))
    idx = np.arange(C, dtype=np.float64)
    diff = idx[:, None] - idx[None, :]
    dm = np.where(diff >= 0, np.exp(log_gamma[:, None, None] * np.maximum(diff, 0.0)), 0.0)
    lane_head = np.arange(W) // RET_HEAD_DIM
    xi = np.exp(log_gamma[lane_head][None, :] * (idx[:, None] + 1.0))
    zeta = np.exp(log_gamma[lane_head][None, :] * (C - 1.0 - idx[:, None]))
    to_pairs = lambda a: np.ascontiguousarray(a.reshape(C, H // 2, LANES).transpose(1, 0, 2))
    row_head = np.arange(LANES) // RET_HEAD_DIM
    smask = (row_head[:, None] == row_head[None, :]).astype(np.float64)
    sdec = np.stack([np.exp(log_gamma[2 * p + row_head] * C)[:, None] * np.ones((1, LANES))
                     for p in range(H // 2)])
    consts = [jnp.asarray(a, F32) for a in (dm, to_pairs(xi), to_pairs(zeta), sdec, smask)]

    tok = lambda: pl.BlockSpec((1, RET_ROWS, W), lambda b, n: (b, n, 0))
    full = lambda a: pl.BlockSpec(a.shape, lambda b, n: (0,) * a.ndim)
    return pl.pallas_call(
        _retention_kernel,
        grid=(B, S // RET_ROWS),
        in_specs=[tok(), tok(), tok(), tok()] + [full(a) for a in consts] + [full(gnw)],
        out_specs=tok(),
        out_shape=jax.ShapeDtypeStruct((B, S, W), BF16),
        scratch_shapes=[pltpu.VMEM((H // 2, LANES, LANES), F32)],
        compiler_params=pltpu.CompilerParams(
            dimension_semantics=("parallel", "arbitrary"), vmem_limit_bytes=VMEM_LIMIT),
        name="retention",
    )(rq, rk, rv, rg, *consts, gnw)


def _mla_kernel(qi_ref, kj_ref, q_ref, k_ref, vt_ref, bias_ref, o_ref, m_ref, acc_ref, s_ref):
    t = pl.program_id(1)
    i = qi_ref[t]
    j = kj_ref[t]
    tq = q_ref.shape[1]
    n_sub = k_ref.shape[1] // TK_SUB

    @pl.when(j == 0)
    def _():
        m_ref[...] = jnp.full_like(m_ref, -jnp.inf)
        acc_ref[...] = jnp.zeros_like(acc_ref)

    def step(diagonal):
        units = [(sub, hd, sub * TK_SUB if diagonal else 0)
                 for sub in range(n_sub) for hd in range(MLA_HEADS)]

        def scores(unit):
            sub, hd, q0 = unit
            sl = slice(hd * HEAD_PAD, (hd + 1) * HEAD_PAD)
            keys = k_ref[0, sub * TK_SUB:(sub + 1) * TK_SUB, sl]
            s = lax.dot_general(keys, q_ref[0, q0:, sl], (((1,), (1,)), ((), ())),
                                preferred_element_type=F32)
            return s + bias_ref[sub, :, q0:] if diagonal else s

        def stage(n):
            s = scores(units[n])
            s_ref[n % 2, :, units[n][2]:] = s
            return jnp.max(s, axis=0, keepdims=True)

        mx_next = stage(0)
        for n, (sub, hd, q0) in enumerate(units):
            mx = mx_next
            if n + 1 < len(units):
                mx_next = stage(n + 1)
            m_prev = m_ref[hd, :, q0:]
            m_new = jnp.maximum(m_prev, mx)
            a = jnp.exp2(m_prev - m_new)
            pr = jnp.exp2(s_ref[n % 2, :, q0:] - m_new[:1]).astype(BF16)
            m_ref[hd, :, q0:] = m_new
            vals_t = vt_ref[0, hd, :, sub * TK_SUB:(sub + 1) * TK_SUB]
            acc_ref[hd, :, q0:] = (a[:1] * acc_ref[hd, :, q0:]
                                   + jnp.dot(vals_t, pr, preferred_element_type=F32))

    @pl.when(j < i)
    def _():
        step(False)

    @pl.when(j == i)
    def _():
        step(True)

    @pl.when(j == i)
    def _():
        for p in range(MLA_HEADS // 2):
            outs = []
            for hd in (2 * p, 2 * p + 1):
                a = acc_ref[hd]
                outs.append((a[:MLA_V_DIM] / a[MLA_V_DIM:2 * MLA_V_DIM]).T)
            o_ref[0, :, p * LANES:(p + 1) * LANES] = jnp.concatenate(outs, axis=1).astype(BF16)


def _mla(q, k, vt):
    B, S, _ = q.shape
    assert TQ == TK and TK % TK_SUB == 0
    pairs = [(i, j) for i in range(S // TQ) for j in range(i + 1)]
    qi = jnp.asarray(np.array([p[0] for p in pairs], np.int32))
    kj = jnp.asarray(np.array([p[1] for p in pairs], np.int32))
    qpos = np.arange(TQ)[None, None, :]
    kpos = np.arange(TK_SUB)[None, :, None] + TK_SUB * np.arange(TK // TK_SUB)[:, None, None]
    bias = jnp.asarray(np.where(kpos <= qpos, 0.0, -np.inf), F32)
    grid_spec = pltpu.PrefetchScalarGridSpec(
        num_scalar_prefetch=2,
        grid=(B, len(pairs)),
        in_specs=[
            pl.BlockSpec((1, TQ, MLA_QK_WIDTH), lambda b, t, qi, kj: (b, qi[t], 0)),
            pl.BlockSpec((1, TK, MLA_QK_WIDTH), lambda b, t, qi, kj: (b, kj[t], 0)),
            pl.BlockSpec((1, MLA_HEADS, HEAD_PAD, TK), lambda b, t, qi, kj: (b, 0, 0, kj[t])),
            pl.BlockSpec(bias.shape, lambda b, t, qi, kj: (0, 0, 0)),
        ],
        out_specs=pl.BlockSpec((1, TQ, MLA_WIDTH), lambda b, t, qi, kj: (b, qi[t], 0)),
        scratch_shapes=[
            pltpu.VMEM((MLA_HEADS, SUBLANES, TQ), F32),
            pltpu.VMEM((MLA_HEADS, HEAD_PAD, TQ), F32),
            pltpu.VMEM((2, TK_SUB, TQ), F32),
        ],
    )
    return pl.pallas_call(
        _mla_kernel,
        grid_spec=grid_spec,
        out_shape=jax.ShapeDtypeStruct((B, S, MLA_WIDTH), BF16),
        compiler_params=pltpu.CompilerParams(
            dimension_semantics=("parallel", "arbitrary"), vmem_limit_bytes=VMEM_LIMIT),
        name="mla_attention",
    )(qi, kj, q, k, vt, bias)


def _ffn_kernel(x_ref, yr_ref, ym_ref, wo_ref, fnw_ref, wup_ref, cw_ref, cb_ref, wdn_ref, onw_ref,
                o_ref, carry_ref, act_ref):
    tm = x_ref.shape[1]

    @pl.when(pl.program_id(1) == 0)
    def _():
        carry_ref[...] = jnp.zeros_like(carry_ref)

    x1 = (x_ref[0]
          + jnp.dot(yr_ref[0], wo_ref[:RET_WIDTH, :], preferred_element_type=F32)
          + jnp.dot(ym_ref[0], wo_ref[RET_WIDTH:, :], preferred_element_type=F32))
    h = _rms(x1, fnw_ref[...]).astype(BF16)

    first_row = lax.broadcasted_iota(jnp.int32, (SUBLANES, TF), 0) == 0

    def shift_down(v, row0):
        r = pltpu.roll(v, 1, 0)
        return jnp.concatenate([jnp.where(first_row, row0, r[:SUBLANES]), r[SUBLANES:]], axis=0)

    def causal_conv(lo):
        cols = slice(lo, lo + TF)
        u = jnp.dot(h, wup_ref[:, cols], preferred_element_type=F32)
        w0, w1, w2 = cw_ref[0:1, cols], cw_ref[1:2, cols], cw_ref[2:3, cols]
        prev = carry_ref[:, cols]
        p2, p1 = prev[SUBLANES - 2:SUBLANES - 1], prev[SUBLANES - 1:SUBLANES]
        carry_ref[:, cols] = u[tm - SUBLANES:]
        v = shift_down(w0 * u, w0 * p1) + w1 * u
        return shift_down(v, w0 * p2 + w1 * p1) + w2 * u + cb_ref[:, cols]

    for f in range(D_FF // TF):
        gate = causal_conv(f * TF)
        val = causal_conv(D_FF + f * TF)
        act_ref[:, f * TF:(f + 1) * TF] = (gate * jax.nn.sigmoid(gate) * val).astype(BF16)

    x2 = x1 + jnp.dot(act_ref[...], wdn_ref[...], preferred_element_type=F32)
    o_ref[0] = _rms(x2, onw_ref[...])


def _ffn(x, y_ret, y_mla, wo, fnw, wup, cw, cb, wdn, onw):
    B, S, D = x.shape
    tm = TM_FFN
    tok = lambda w: pl.BlockSpec((1, tm, w), lambda b, i: (b, i, 0))
    full = lambda a: pl.BlockSpec(a.shape, lambda b, i: (0,) * a.ndim, pipeline_mode=pl.Buffered(1))
    return pl.pallas_call(
        _ffn_kernel,
        grid=(B, S // tm),
        in_specs=[tok(D), tok(RET_WIDTH), tok(MLA_WIDTH)] + [full(a) for a in (wo, fnw, wup, cw, cb, wdn, onw)],
        out_specs=tok(D),
        out_shape=jax.ShapeDtypeStruct((B, S, D), F32),
        scratch_shapes=[
            pltpu.VMEM((SUBLANES, 2 * D_FF), F32),
            pltpu.VMEM((tm, D_FF), BF16),
        ],
        compiler_params=pltpu.CompilerParams(
            dimension_semantics=("parallel", "arbitrary"), vmem_limit_bytes=VMEM_LIMIT),
        name="outproj_ffn",
    )(x, y_ret, y_mla, wo, fnw, wup, cw, cb, wdn, onw)


def _rotary_lanes():
    lane = np.arange(LANES)

    def rows(d, active, first_half, freq_idx):
        inv_freq = ROPE_BASE ** (-jnp.arange(0, d, 2, dtype=F32) / d)
        freq = jnp.where(active, inv_freq[freq_idx], 0.0)
        return freq, jnp.asarray(np.where(first_half, -1.0, 1.0), F32)

    half_r = RET_HEAD_DIM // 2
    fr, gr = rows(RET_HEAD_DIM, np.ones(LANES, bool), (lane % RET_HEAD_DIM) < half_r, lane % half_r)
    half_m = MLA_ROPE_DIM // 2
    in_rope = (lane >= MLA_NOPE_DIM) & (lane < MLA_NOPE_DIM + MLA_ROPE_DIM)
    fm, gm = rows(MLA_ROPE_DIM, in_rope, lane < MLA_NOPE_DIM + half_m, (lane - MLA_NOPE_DIM) % half_m)
    return jnp.stack([fr, gr, fm, gm]).astype(F32)


def _layer(x, pos, rot, attn_norm_w, w_in, ret_gn_w, mla_q_norm_w, w_uq, mla_kv_norm_w, w_ukv,
           w_out, ffn_norm_w, w_up, conv_w, conv_b, w_down, out_norm_w):
    D = x.shape[-1]
    qk = MLA_NOPE_DIM + MLA_ROPE_DIM
    kpe_cols = jnp.zeros((D, LANES), F32).at[:, MLA_NOPE_DIM:qk].set(w_in[:, -MLA_ROPE_DIM:])
    win = jnp.concatenate([w_in[:, :-MLA_ROPE_DIM], kpe_cols], axis=1).astype(BF16)
    wq = jnp.pad(w_uq.reshape(MLA_Q_RANK, MLA_HEADS, qk), ((0, 0), (0, 0), (0, HEAD_PAD - qk)))
    wq = wq.reshape(MLA_Q_RANK, MLA_QK_WIDTH).astype(BF16)
    wkv = w_ukv.reshape(MLA_KV_RANK, MLA_HEADS, MLA_NOPE_DIM + MLA_V_DIM)
    wk = jnp.pad(wkv[:, :, :MLA_NOPE_DIM], ((0, 0), (0, 0), (0, HEAD_PAD - MLA_NOPE_DIM)))
    wk = wk.reshape(MLA_KV_RANK, MLA_QK_WIDTH).astype(BF16)
    wv = wkv[:, :, MLA_NOPE_DIM:].reshape(MLA_KV_RANK, MLA_WIDTH).T.astype(BF16)
    row = lambda a: a.reshape(1, -1).astype(F32)

    rq, rk, rv, rg, q, k, v = _inproj(x, pos, rot, row(attn_norm_w), win, row(mla_q_norm_w), wq,
                                      row(mla_kv_norm_w), wk, wv)
    y_ret = _retention(rq, rk, rv, rg, row(ret_gn_w))
    y_mla = _mla(q, k, v)
    return _ffn(x, y_ret, y_mla, w_out.astype(BF16), row(ffn_norm_w), w_up.astype(BF16),
                conv_w.astype(F32), row(conv_b), w_down.astype(BF16), row(out_norm_w))


def kernel(x, positions, attn_norm_w, w_in, ret_gn_w, mla_q_norm_w, w_uq, mla_kv_norm_w, w_ukv,
           w_out, ffn_norm_w, w_up, conv_w, conv_b, w_down, final_norm_w):
    depth = w_in.shape[0]
    assert depth == 1, "the final RMSNorm is fused into the (single) layer's FFN kernel"
    pos = jnp.broadcast_to(positions.astype(F32)[..., None], positions.shape + (LANES,))
    return _layer(x, pos, _rotary_lanes(), attn_norm_w[0], w_in[0], ret_gn_w[0], mla_q_norm_w[0], w_uq[0],
                  mla_kv_norm_w[0], w_ukv[0], w_out[0], ffn_norm_w[0], w_up[0], conv_w[0],
                  conv_b[0], w_down[0], final_norm_w)
```

```python
import jax
import jax.numpy as jnp
import numpy as np
from jax import lax
from jax.experimental import pallas as pl
from jax.experimental.pallas import tpu as pltpu

F32 = jnp.float32
BF16 = jnp.bfloat16

D_MODEL = 1024
RET_HEADS = 8
RET_HEAD_DIM = 64
RET_WIDTH = RET_HEADS * RET_HEAD_DIM
MLA_HEADS = 8
MLA_NOPE_DIM = 64
MLA_ROPE_DIM = 32
MLA_V_DIM = 64
MLA_Q_RANK = 256
MLA_KV_RANK = 128
MLA_WIDTH = MLA_HEADS * MLA_V_DIM
D_FF = 2816
CONV_WIDTH = 3
ROPE_BASE = 10000.0
EPS = 1e-6
LOG2_E = 1.4426950408889634

LANES = 128
SUBLANES = 8
HEAD_PAD = LANES
MLA_QK_WIDTH = MLA_HEADS * HEAD_PAD
IN_PAD_WIDTH = 4 * RET_WIDTH + MLA_Q_RANK + MLA_KV_RANK + LANES

TM_IN = 512
RET_CHUNK = 256
RET_ROWS = 1024
TQ = 1024
TK = 1024
TK_SUB = 512
TM_FFN = 512
TF = 256
VMEM_LIMIT = 56 * 1024 * 1024


def _rms(x, w):
    return x * lax.rsqrt(jnp.mean(x * x, axis=-1, keepdims=True) + EPS) * w


def _inproj_kernel(x_ref, pos_ref, rot_ref, anw_ref, win_ref, qnw_ref, wq_ref,
                   kvnw_ref, wk_ref, wvt_ref,
                   rq_ref, rk_ref, rv_ref, rg_ref, q_ref, k_ref, vt_ref):
    tm = x_ref.shape[1]
    h = _rms(x_ref[0], anw_ref[...]).astype(BF16)

    lane = lax.broadcasted_iota(jnp.int32, (tm, LANES), 1)
    half = RET_HEAD_DIM // 2
    first_half = (lane % RET_HEAD_DIM) < half
    ang = pos_ref[0] * rot_ref[0:1]
    cr, sr = jnp.cos(ang), jnp.sin(ang) * rot_ref[1:2]
    mla_rope = (lane >= MLA_NOPE_DIM) & (lane % half < MLA_ROPE_DIM // 2)
    cm, sm = jnp.where(mla_rope, cr, 1.0), jnp.where(mla_rope, sr, 0.0)

    def rotate(p, cos, sin):
        rot = jnp.where(first_half, pltpu.roll(p, LANES - half, 1), pltpu.roll(p, half, 1))
        return p * cos + rot * sin

    rope_ret = lambda p: rotate(p, cr, sr)
    rope_mla = lambda p: rotate(p, cm, sm)

    def seg(lo, width):
        return jnp.dot(h, win_ref[:, lo:lo + width], preferred_element_type=F32)

    cq = seg(4 * RET_WIDTH, MLA_Q_RANK)
    tail = seg(4 * RET_WIDTH + MLA_Q_RANK, MLA_KV_RANK + LANES)
    pq = seg(0, RET_WIDTH)
    cqn = _rms(cq, qnw_ref[...]).astype(BF16)
    ckv_norm = _rms(tail[:, :MLA_KV_RANK], kvnw_ref[...])
    ckvn = ckv_norm.astype(BF16)
    ckvn_t = ckv_norm.T.astype(BF16)
    kpe = rope_mla(tail[:, MLA_KV_RANK:])
    pk = seg(RET_WIDTH, RET_WIDTH)
    qf = jnp.dot(cqn, wq_ref[...], preferred_element_type=F32)
    kf = jnp.dot(ckvn, wk_ref[...], preferred_element_type=F32)
    vt = jnp.dot(wvt_ref[...], ckvn_t, preferred_element_type=F32)
    rv_ref[0] = seg(2 * RET_WIDTH, RET_WIDTH).astype(BF16)
    rg_ref[0] = seg(3 * RET_WIDTH, RET_WIDTH)
    for g in range(RET_WIDTH // LANES):
        sl = slice(g * LANES, (g + 1) * LANES)
        rq_ref[0, :, sl] = rope_ret(pq[:, sl]).astype(BF16)
        rk_ref[0, :, sl] = (rope_ret(pk[:, sl]) * (RET_HEAD_DIM ** -0.5)).astype(BF16)
    scale = (MLA_NOPE_DIM + MLA_ROPE_DIM) ** -0.5 * LOG2_E
    for hd in range(MLA_HEADS):
        sl = slice(hd * HEAD_PAD, (hd + 1) * HEAD_PAD)
        q_ref[0, :, sl] = (rope_mla(qf[:, sl]) * scale).astype(BF16)
        k_ref[0, :, sl] = (kf[:, sl] + kpe).astype(BF16)
        vt_ref[0, hd, :MLA_V_DIM, :] = vt[hd * MLA_V_DIM:(hd + 1) * MLA_V_DIM].astype(BF16)
        vt_ref[0, hd, MLA_V_DIM:, :] = jnp.ones((HEAD_PAD - MLA_V_DIM, tm), BF16)


def _inproj(x, pos, rot, anw, win, qnw, wq, kvnw, wk, wv):
    B, S, D = x.shape
    tm = TM_IN
    grid = (B, S // tm)
    tok = lambda w: pl.BlockSpec((1, tm, w), lambda b, i: (b, i, 0))
    full = lambda a: pl.BlockSpec(a.shape, lambda b, i: (0,) * a.ndim)
    outs = [
        jax.ShapeDtypeStruct((B, S, RET_WIDTH), BF16),
        jax.ShapeDtypeStruct((B, S, RET_WIDTH), BF16),
        jax.ShapeDtypeStruct((B, S, RET_WIDTH), BF16),
        jax.ShapeDtypeStruct((B, S, RET_WIDTH), F32),
        jax.ShapeDtypeStruct((B, S, MLA_QK_WIDTH), BF16),
        jax.ShapeDtypeStruct((B, S, MLA_QK_WIDTH), BF16),
        jax.ShapeDtypeStruct((B, MLA_HEADS, HEAD_PAD, S), BF16),
    ]
    vt_spec = pl.BlockSpec((1, MLA_HEADS, HEAD_PAD, tm), lambda b, i: (b, 0, 0, i))
    return pl.pallas_call(
        _inproj_kernel,
        grid=grid,
        in_specs=[tok(D), tok(LANES)] + [full(a) for a in (rot, anw, win, qnw, wq, kvnw, wk, wv)],
        out_specs=[tok(o.shape[-1]) for o in outs[:-1]] + [vt_spec],
        out_shape=outs,
        compiler_params=pltpu.CompilerParams(
            dimension_semantics=("parallel", "parallel"), vmem_limit_bytes=VMEM_LIMIT),
        name="inproj",
    )(x, pos, rot, anw, win, qnw, wq, kvnw, wk, wv)


def _retention_kernel(q_ref, k_ref, v_ref, g_ref, dm_ref, xi_ref, zeta_ref, sdec_ref, smask_ref,
                      gnw_ref, o_ref, r_ref):
    C = RET_CHUNK

    @pl.when(pl.program_id(1) == 0)
    def _():
        r_ref[...] = jnp.zeros_like(r_ref)

    lane = lax.broadcasted_iota(jnp.int32, (C, LANES), 1)
    first = lane < RET_HEAD_DIM
    inv_n = 1.0 / RET_HEAD_DIM
    for c, p in [(c, p) for c in range(q_ref.shape[1] // C) for p in range(RET_HEADS // 2)]:
        rows = slice(c * C, (c + 1) * C)
        sl = slice(p * LANES, (p + 1) * LANES)
        q = q_ref[0, rows, sl]
        k = k_ref[0, rows, sl]
        v = v_ref[0, rows, sl]
        zero = jnp.zeros_like(q)
        inner = []
        for hh, qm in enumerate((jnp.where(first, q, zero), jnp.where(first, zero, q))):
            s = lax.dot_general(qm, k, (((1,), (1,)), ((), ())), preferred_element_type=F32)
            s = (s * dm_ref[2 * p + hh]).astype(BF16)
            inner.append(jnp.dot(s, v, preferred_element_type=F32))
        o = jnp.where(first, inner[0], inner[1])
        r_prev = r_ref[p]
        o = o + jnp.dot(q, r_prev.astype(BF16), preferred_element_type=F32) * xi_ref[p]
        kz = (k.astype(F32) * zeta_ref[p]).T.astype(BF16)
        upd = jnp.dot(kz, v, preferred_element_type=F32)
        r_ref[p] = sdec_ref[p] * r_prev + smask_ref[...] * upd

        s_all = jnp.sum(o, axis=-1, keepdims=True)
        s_1 = jnp.sum(jnp.where(first, o, 0.0), axis=-1, keepdims=True)
        mu = jnp.where(first, s_1, s_all - s_1) * inv_n
        d = o - mu
        d2 = d * d
        v_all = jnp.sum(d2, axis=-1, keepdims=True)
        v_1 = jnp.sum(jnp.where(first, d2, 0.0), axis=-1, keepdims=True)
        var = jnp.where(first, v_1, v_all - v_1) * inv_n
        y = d * lax.rsqrt(var + EPS) * gnw_ref[:, sl]
        g = g_ref[0, rows, sl]
        o_ref[0, rows, sl] = (g * jax.nn.sigmoid(g) * y).astype(BF16)


def _retention(rq, rk, rv, rg, gnw):
    B, S, W = rq.shape
    C = RET_CHUNK
    H = RET_HEADS
    log_gamma = np.log1p(-np.power(2.0, -5.0 - np.arange(H, dtype=np.float64)))
    idx = np.arange(C, dtype=np.float64)
    diff = idx[:, None] - idx[None, :]
    dm = np.where(diff >= 0, np.exp(log_gamma[:, None, None] * np.maximum(diff, 0.0)), 0.0)
    lane_head = np.arange(W) // RET_HEAD_DIM
    xi = np.exp(log_gamma[lane_head][None, :] * (idx[:, None] + 1.0))
    zeta = np.exp(log_gamma[lane_head][None, :] * (C - 1.0 - idx[:, None]))
    to_pairs = lambda a: np.ascontiguousarray(a.reshape(C, H // 2, LANES).transpose(1, 0, 2))
    row_head = np.arange(LANES) // RET_HEAD_DIM
    smask = (row_head[:, None] == row_head[None, :]).astype(np.float64)
    sdec = np.stack([np.exp(log_gamma[2 * p + row_head] * C)[:, None] * np.ones((1, LANES))
                     for p in range(H // 2)])
    consts = [jnp.asarray(a, F32) for a in (dm, to_pairs(xi), to_pairs(zeta), sdec, smask)]

    tok = lambda: pl.BlockSpec((1, RET_ROWS, W), lambda b, n: (b, n, 0))
    full = lambda a: pl.BlockSpec(a.shape, lambda b, n: (0,) * a.ndim)
    return pl.pallas_call(
        _retention_kernel,
        grid=(B, S // RET_ROWS),
        in_specs=[tok(), tok(), tok(), tok()] + [full(a) for a in consts] + [full(gnw)],
        out_specs=tok(),
        out_shape=jax.ShapeDtypeStruct((B, S, W), BF16),
        scratch_shapes=[pltpu.VMEM((H // 2, LANES, LANES), F32)],
        compiler_params=pltpu.CompilerParams(
            dimension_semantics=("parallel", "arbitrary"), vmem_limit_bytes=VMEM_LIMIT),
        name="retention",
    )(rq, rk, rv, rg, *consts, gnw)


def _mla_kernel(qi_ref, kj_ref, q_ref, k_ref, vt_ref, bias_ref, o_ref, m_ref, acc_ref, s_ref):
    t = pl.program_id(1)
    i = qi_ref[t]
    j = kj_ref[t]
    tq = q_ref.shape[1]
    n_sub = k_ref.shape[1] // TK_SUB

    @pl.when(j == 0)
    def _():
        m_ref[...] = jnp.full_like(m_ref, -jnp.inf)
        acc_ref[...] = jnp.zeros_like(acc_ref)

    def step(diagonal):
        units = [(sub, hd, sub * TK_SUB if diagonal else 0)
                 for sub in range(n_sub) for hd in range(MLA_HEADS)]

        def scores(unit):
            sub, hd, q0 = unit
            sl = slice(hd * HEAD_PAD, (hd + 1) * HEAD_PAD)
            keys = k_ref[0, sub * TK_SUB:(sub + 1) * TK_SUB, sl]
            s = lax.dot_general(keys, q_ref[0, q0:, sl], (((1,), (1,)), ((), ())),
                                preferred_element_type=F32)
            return s + bias_ref[sub, :, q0:] if diagonal else s

        def stage(n):
            s = scores(units[n])
            s_ref[n % 2, :, units[n][2]:] = s
            return jnp.max(s, axis=0, keepdims=True)

        mx_next = stage(0)
        for n, (sub, hd, q0) in enumerate(units):
            mx = mx_next
            if n + 1 < len(units):
                mx_next = stage(n + 1)
            m_prev = m_ref[hd, :, q0:]
            m_new = jnp.maximum(m_prev, mx)
            a = jnp.exp2(m_prev - m_new)
            pr = jnp.exp2(s_ref[n % 2, :, q0:] - m_new[:1]).astype(BF16)
            m_ref[hd, :, q0:] = m_new
            vals_t = vt_ref[0, hd, :, sub * TK_SUB:(sub + 1) * TK_SUB]
            acc_ref[hd, :, q0:] = (a[:1] * acc_ref[hd, :, q0:]
                                   + jnp.dot(vals_t, pr, preferred_element_type=F32))

    @pl.when(j < i)
    def _():
        step(False)

    @pl.when(j == i)
    def _():
        step(True)

    @pl.when(j == i)
    def _():
        for p in range(MLA_HEADS // 2):
            outs = []
            for hd in (2 * p, 2 * p + 1):
                a = acc_ref[hd]
                outs.append((a[:MLA_V_DIM] / a[MLA_V_DIM:2 * MLA_V_DIM]).T)
            o_ref[0, :, p * LANES:(p + 1) * LANES] = jnp.concatenate(outs, axis=1).astype(BF16)


def _mla(q, k, vt):
    B, S, _ = q.shape
    assert TQ == TK and TK % TK_SUB == 0
    pairs = [(i, j) for i in range(S // TQ) for j in range(i + 1)]
    qi = jnp.asarray(np.array([p[0] for p in pairs], np.int32))
    kj = jnp.asarray(np.array([p[1] for p in pairs], np.int32))
    qpos = np.arange(TQ)[None, None, :]
    kpos = np.arange(TK_SUB)[None, :, None] + TK_SUB * np.arange(TK // TK_SUB)[:, None, None]
    bias = jnp.asarray(np.where(kpos <= qpos, 0.0, -np.inf), F32)
    grid_spec = pltpu.PrefetchScalarGridSpec(
        num_scalar_prefetch=2,
        grid=(B, len(pairs)),
        in_specs=[
            pl.BlockSpec((1, TQ, MLA_QK_WIDTH), lambda b, t, qi, kj: (b, qi[t], 0)),
            pl.BlockSpec((1, TK, MLA_QK_WIDTH), lambda b, t, qi, kj: (b, kj[t], 0)),
            pl.BlockSpec((1, MLA_HEADS, HEAD_PAD, TK), lambda b, t, qi, kj: (b, 0, 0, kj[t])),
            pl.BlockSpec(bias.shape, lambda b, t, qi, kj: (0, 0, 0)),
        ],
        out_specs=pl.BlockSpec((1, TQ, MLA_WIDTH), lambda b, t, qi, kj: (b, qi[t], 0)),
        scratch_shapes=[
            pltpu.VMEM((MLA_HEADS, SUBLANES, TQ), F32),
            pltpu.VMEM((MLA_HEADS, HEAD_PAD, TQ), F32),
            pltpu.VMEM((2, TK_SUB, TQ), F32),
        ],
    )
    return pl.pallas_call(
        _mla_kernel,
        grid_spec=grid_spec,
        out_shape=jax.ShapeDtypeStruct((B, S, MLA_WIDTH), BF16),
        compiler_params=pltpu.CompilerParams(
            dimension_semantics=("parallel", "arbitrary"), vmem_limit_bytes=VMEM_LIMIT),
        name="mla_attention",
    )(qi, kj, q, k, vt, bias)


def _ffn_kernel(x_ref, yr_ref, ym_ref, wo_ref, fnw_ref, wup_ref, cw_ref, cb_ref, wdn_ref, onw_ref,
                o_ref, carry_ref, act_ref):
    tm = x_ref.shape[1]

    @pl.when(pl.program_id(1) == 0)
    def _():
        carry_ref[...] = jnp.zeros_like(carry_ref)

    x1 = (x_ref[0]
          + jnp.dot(yr_ref[0], wo_ref[:RET_WIDTH, :], preferred_element_type=F32)
          + jnp.dot(ym_ref[0], wo_ref[RET_WIDTH:, :], preferred_element_type=F32))
    h = _rms(x1, fnw_ref[...]).astype(BF16)

    first_row = lax.broadcasted_iota(jnp.int32, (SUBLANES, TF), 0) == 0

    def shift_down(v, row0):
        r = pltpu.roll(v, 1, 0)
        return jnp.concatenate([jnp.where(first_row, row0, r[:SUBLANES]), r[SUBLANES:]], axis=0)

    def causal_conv(lo):
        cols = slice(lo, lo + TF)
        u = jnp.dot(h, wup_ref[:, cols], preferred_element_type=F32)
        w0, w1, w2 = cw_ref[0:1, cols], cw_ref[1:2, cols], cw_ref[2:3, cols]
        prev = carry_ref[:, cols]
        p2, p1 = prev[SUBLANES - 2:SUBLANES - 1], prev[SUBLANES - 1:SUBLANES]
        carry_ref[:, cols] = u[tm - SUBLANES:]
        v = shift_down(w0 * u, w0 * p1) + w1 * u
        return shift_down(v, w0 * p2 + w1 * p1) + w2 * u + cb_ref[:, cols]

    for f in range(D_FF // TF):
        gate = causal_conv(f * TF)
        val = causal_conv(D_FF + f * TF)
        act_ref[:, f * TF:(f + 1) * TF] = (gate * jax.nn.sigmoid(gate) * val).astype(BF16)

    x2 = x1 + jnp.dot(act_ref[...], wdn_ref[...], preferred_element_type=F32)
    o_ref[0] = _rms(x2, onw_ref[...])


def _ffn(x, y_ret, y_mla, wo, fnw, wup, cw, cb, wdn, onw):
    B, S, D = x.shape
    tm = TM_FFN
    tok = lambda w: pl.BlockSpec((1, tm, w), lambda b, i: (b, i, 0))
    full = lambda a: pl.BlockSpec(a.shape, lambda b, i: (0,) * a.ndim, pipeline_mode=pl.Buffered(1))
    return pl.pallas_call(
        _ffn_kernel,
        grid=(B, S // tm),
        in_specs=[tok(D), tok(RET_WIDTH), tok(MLA_WIDTH)] + [full(a) for a in (wo, fnw, wup, cw, cb, wdn, onw)],
        out_specs=tok(D),
        out_shape=jax.ShapeDtypeStruct((B, S, D), F32),
        scratch_shapes=[
            pltpu.VMEM((SUBLANES, 2 * D_FF), F32),
            pltpu.VMEM((tm, D_FF), BF16),
        ],
        compiler_params=pltpu.CompilerParams(
            dimension_semantics=("parallel", "arbitrary"), vmem_limit_bytes=VMEM_LIMIT),
        name="outproj_ffn",
    )(x, y_ret, y_mla, wo, fnw, wup, cw, cb, wdn, onw)


def _rotary_layout():
    half_r, half_m = RET_HEAD_DIM // 2, MLA_ROPE_DIM // 2
    assert MLA_NOPE_DIM == RET_HEAD_DIM and half_m * 2 == half_r and HEAD_PAD == 2 * RET_HEAD_DIM
    order = np.concatenate([np.arange(0, half_r, 2), np.arange(1, half_r, 2)])
    head = np.concatenate([order, half_r + order])
    ret_perm = (np.arange(RET_HEADS)[:, None] * RET_HEAD_DIM + head[None, :]).reshape(-1)
    rope_dim = np.arange(MLA_ROPE_DIM)
    mla_lane = MLA_NOPE_DIM + np.where(rope_dim < half_m, rope_dim, half_r + rope_dim - half_m)
    inv_freq = ROPE_BASE ** (-jnp.arange(0, RET_HEAD_DIM, 2, dtype=F32) / RET_HEAD_DIM)
    lane = np.arange(LANES)
    sign = np.where(lane % RET_HEAD_DIM < half_r, -1.0, 1.0)
    rot = jnp.stack([inv_freq[order[lane % half_r]], jnp.asarray(sign, F32)])
    return ret_perm, mla_lane, rot


def _layer(x, pos, attn_norm_w, w_in, ret_gn_w, mla_q_norm_w, w_uq, mla_kv_norm_w, w_ukv,
           w_out, ffn_norm_w, w_up, conv_w, conv_b, w_down, out_norm_w):
    D = x.shape[-1]
    qk = MLA_NOPE_DIM + MLA_ROPE_DIM
    ret_perm, mla_lane, rot = _rotary_layout()
    kpe_cols = jnp.zeros((D, LANES), F32).at[:, mla_lane].set(w_in[:, -MLA_ROPE_DIM:])
    win = jnp.concatenate([w_in[:, :RET_WIDTH][:, ret_perm], w_in[:, RET_WIDTH:2 * RET_WIDTH][:, ret_perm],
                           w_in[:, 2 * RET_WIDTH:-MLA_ROPE_DIM], kpe_cols], axis=1).astype(BF16)
    wq3 = w_uq.reshape(MLA_Q_RANK, MLA_HEADS, qk)
    wq = jnp.zeros((MLA_Q_RANK, MLA_HEADS, HEAD_PAD), F32).at[:, :, :MLA_NOPE_DIM].set(wq3[:, :, :MLA_NOPE_DIM])
    wq = wq.at[:, :, mla_lane].set(wq3[:, :, MLA_NOPE_DIM:])
    wq = wq.reshape(MLA_Q_RANK, MLA_QK_WIDTH).astype(BF16)
    wkv = w_ukv.reshape(MLA_KV_RANK, MLA_HEADS, MLA_NOPE_DIM + MLA_V_DIM)
    wk = jnp.pad(wkv[:, :, :MLA_NOPE_DIM], ((0, 0), (0, 0), (0, HEAD_PAD - MLA_NOPE_DIM)))
    wk = wk.reshape(MLA_KV_RANK, MLA_QK_WIDTH).astype(BF16)
    wv = wkv[:, :, MLA_NOPE_DIM:].reshape(MLA_KV_RANK, MLA_WIDTH).T.astype(BF16)
    row = lambda a: a.reshape(1, -1).astype(F32)

    rq, rk, rv, rg, q, k, v = _inproj(x, pos, rot, row(attn_norm_w), win, row(mla_q_norm_w), wq,
                                      row(mla_kv_norm_w), wk, wv)
    y_ret = _retention(rq, rk, rv, rg, row(ret_gn_w))
    y_mla = _mla(q, k, v)
    return _ffn(x, y_ret, y_mla, w_out.astype(BF16), row(ffn_norm_w), w_up.astype(BF16),
                conv_w.astype(F32), row(conv_b), w_down.astype(BF16), row(out_norm_w))


def kernel(x, positions, attn_norm_w, w_in, ret_gn_w, mla_q_norm_w, w_uq, mla_kv_norm_w, w_ukv,
           w_out, ffn_norm_w, w_up, conv_w, conv_b, w_down, final_norm_w):
    depth = w_in.shape[0]
    assert depth == 1, "the final RMSNorm is fused into the (single) layer's FFN kernel"
    pos = jnp.broadcast_to(positions.astype(F32)[..., None], positions.shape + (LANES,))
    return _layer(x, pos, attn_norm_w[0], w_in[0], ret_gn_w[0], mla_q_norm_w[0], w_uq[0],
                  mla_kv_norm_w[0], w_ukv[0], w_out[0], ffn_norm_w[0], w_up[0], conv_w[0],
                  conv_b[0], w_down[0], final_norm_w)
```
